```python
import jax, jax.numpy as jnp
from jax import lax
import numpy as np

D_MODEL = 2048
BATCH = 16
SEQ = 2048
DEPTH = 4

HEAD_DIM = 128
GDN_HEADS = 6
GDN_CHUNK = 64
CONV_K = 4
SG_GROUPS = 4
SG_CHUNK = 128
SA_HEADS = 6
IDX_HEADS = 16
IDX_DIM = 64
INDEX_TOPK = 256
Q_BLOCK = 128
MEM_LEN = 256
MEM_HEADS = 4
D_FF = 4 * D_MODEL
ROPE_THETA = 10000.0
NORM_EPS = 1e-6

D_A = GDN_HEADS * HEAD_DIM
D_B = SG_GROUPS * HEAD_DIM
D_C = SA_HEADS * HEAD_DIM
D_MIX = D_A + D_B + D_C
D_MEM = MEM_HEADS * HEAD_DIM
IN_WIDTHS = (3 * D_A, D_A, GDN_HEADS, GDN_HEADS, D_B, D_B, D_C, HEAD_DIM, HEAD_DIM,
             IDX_HEADS * IDX_DIM, IDX_DIM, IDX_HEADS)
D_IN = sum(IN_WIDTHS)
IN_SPLIT_POINTS = tuple(int(v) for v in np.cumsum(IN_WIDTHS)[:-1])

kernel_name = "hybrid_gdn_sgmlp_dsa_trunk"


def rms_norm(x, g):
    xf = x.astype(jnp.float32)
    y = xf * lax.rsqrt(jnp.mean(xf * xf, axis=-1, keepdims=True) + NORM_EPS)
    return (y * g.astype(jnp.float32)).astype(x.dtype)


def layer_norm(x, g, b):
    xf = x.astype(jnp.float32)
    mu = jnp.mean(xf, axis=-1, keepdims=True)
    var = jnp.mean(jnp.square(xf - mu), axis=-1, keepdims=True)
    y = (xf - mu) * lax.rsqrt(var + NORM_EPS) * g.astype(jnp.float32) + b.astype(jnp.float32)
    return y.astype(x.dtype)


def l2_normalize(x):
    xf = x.astype(jnp.float32)
    return xf * lax.rsqrt(jnp.sum(xf * xf, axis=-1, keepdims=True) + NORM_EPS)


def rope_tables(seq_len, dim):
    inv_freq = ROPE_THETA ** (-jnp.arange(0, dim, 2, dtype=jnp.float32) / dim)
    ang = jnp.arange(seq_len, dtype=jnp.float32)[:, None] * inv_freq[None, :]
    return jnp.cos(ang), jnp.sin(ang)


def apply_rope(x, cos, sin):
    half = x.shape[-1] // 2
    shape = (1, x.shape[1]) + (1,) * (x.ndim - 3) + (half,)
    c = cos.reshape(shape).astype(x.dtype)
    s = sin.reshape(shape).astype(x.dtype)
    x1, x2 = x[..., :half], x[..., half:]
    return jnp.concatenate([x1 * c - x2 * s, x2 * c + x1 * s], axis=-1)


def causal_depthwise_conv(x, w):
    c = x.shape[-1]
    return lax.conv_general_dilated(
        x, w.astype(x.dtype)[:, None, :], window_strides=(1,), padding=((CONV_K - 1, 0),),
        dimension_numbers=("NWC", "WIO", "NWC"), feature_group_count=c)


def gated_delta_rule_chunked(q, k, v, g, beta):
    b_, s_, h_, dk = q.shape
    dv = v.shape[-1]
    c_ = GDN_CHUNK
    n_ = s_ // c_

    def chunk(t):
        t = t.reshape((b_, n_, c_, h_) + t.shape[3:])
        return jnp.moveaxis(t, 3, 1)

    q = chunk(q) * (dk ** -0.5)
    k, v, g, beta = chunk(k), chunk(v), chunk(g), chunk(beta)
    g = jnp.cumsum(g, axis=-1)
    causal = jnp.tril(jnp.ones((c_, c_), dtype=bool))
    decay = jnp.exp(jnp.where(causal, g[..., :, None] - g[..., None, :], -jnp.inf))
    kb = k * beta[..., None]
    strict = jnp.tril(jnp.einsum('bhncd,bhnsd->bhncs', kb, k) * decay, -1)
    eye = jnp.eye(c_, dtype=jnp.float32)
    rhs = jnp.concatenate([v * beta[..., None], kb * jnp.exp(g)[..., None]], axis=-1)
    sol = lax.linalg.triangular_solve(strict + eye, rhs, left_side=True, lower=True,
                                      unit_diagonal=True)
    u, w = sol[..., :dv], sol[..., dv:]
    attn_intra = jnp.einsum('bhncd,bhnsd->bhncs', q, k) * decay
    q_dec = q * jnp.exp(g)[..., None]
    k_dec = k * jnp.exp(g[..., -1:] - g)[..., None]
    g_last = jnp.exp(g[..., -1])

    def step(state, xs):
        u_c, w_c, qd_c, kd_c, a_c, gl_c = xs
        v_new = u_c - jnp.matmul(w_c, state)
        o_c = jnp.matmul(qd_c, state) + jnp.matmul(a_c, v_new)
        state = state * gl_c[..., None, None] + jnp.matmul(jnp.swapaxes(kd_c, -1, -2), v_new)
        return state, o_c

    xs = tuple(jnp.moveaxis(t, 2, 0) for t in (u, w, q_dec, k_dec, attn_intra, g_last))
    state0 = jnp.zeros((b_, h_, dk, dv), jnp.float32)
    _, o = lax.scan(step, state0, xs)
    return jnp.transpose(o, (1, 0, 3, 2, 4)).reshape(b_, s_, h_, dv)


def spatial_gating(z_u, z_v, ln_g, ln_b, w_s, b_s):
    b_, s_, _ = z_v.shape
    v = layer_norm(z_v, ln_g, ln_b)
    v = v.reshape(b_, s_ // SG_CHUNK, SG_CHUNK, SG_GROUPS, HEAD_DIM)
    mask = jnp.tril(jnp.ones((SG_CHUNK, SG_CHUNK), dtype=bool))
    w = jnp.where(mask[None], w_s, 0.0).astype(v.dtype)
    s = jnp.einsum('gts,bcsgd->bctgd', w, v) + b_s.T.astype(v.dtype)[None, None, :, :, None]
    return z_u * s.reshape(b_, s_, D_B)


def indexer_sparse_attention(q, k, v, iq, ik, iw):
    b_, s_ = q.shape[:2]
    n_sel = min(INDEX_TOPK, s_ // 4)
    nb = s_ // Q_BLOCK
    key_pos = jnp.arange(s_)

    def blocks(t):
        return jnp.moveaxis(t.reshape((b_, nb, Q_BLOCK) + t.shape[2:]), 1, 0)

    def one_block(args):
        qb, iqb, iwb, tb = args
        rel = jax.nn.relu(jnp.einsum('bqhd,bsd->bqhs', iqb, ik).astype(jnp.float32))
        score = jnp.einsum('bqhs,bqh->bqs', rel, iwb.astype(jnp.float32))
        admissible = key_pos[None, :] <= tb[:, None]
        score = jnp.where(admissible[None], score, -jnp.inf)
        _, idx = lax.top_k(score, n_sel)
        k_sel = jax.vmap(lambda kk, ii: kk[ii])(k, idx)
        v_sel = jax.vmap(lambda vv, ii: vv[ii])(v, idx)
        logits = jnp.einsum('bqhd,bqkd->bqhk', qb, k_sel).astype(jnp.float32) * (HEAD_DIM ** -0.5)
        valid = idx <= tb[None, :, None]
        logits = jnp.where(valid[:, :, None, :], logits, -jnp.inf)
        p = jax.nn.softmax(logits, axis=-1).astype(v.dtype)
        return jnp.einsum('bqhk,bqkd->bqhd', p, v_sel)

    t_pos = jnp.arange(s_).reshape(nb, Q_BLOCK)
    out = lax.map(one_block, (blocks(q), blocks(iq), blocks(iw), t_pos))
    return jnp.moveaxis(out, 0, 1).reshape(b_, s_, SA_HEADS, HEAD_DIM)


def hybrid_mixer(xn, w_in, conv_w, a_log, dt_bias, gdn_norm, sg_ln_g, sg_ln_b, sg_w, sg_b, w_out,
                 cos_h, sin_h, cos_i, sin_i):
    b_, s_, _ = xn.shape
    h = xn @ w_in
    (a_qkv, a_z, a_dec, a_beta, b_u, b_v, c_q, c_k, c_v, i_q, i_k, i_w) = jnp.split(
        h, IN_SPLIT_POINTS, axis=-1)

    qkv = jax.nn.silu(causal_depthwise_conv(a_qkv, conv_w))
    gq, gk, gv = jnp.split(qkv, 3, axis=-1)
    gq = l2_normalize(gq.reshape(b_, s_, GDN_HEADS, HEAD_DIM))
    gk = l2_normalize(gk.reshape(b_, s_, GDN_HEADS, HEAD_DIM))
    gv = gv.reshape(b_, s_, GDN_HEADS, HEAD_DIM).astype(jnp.float32)
    beta = jax.nn.sigmoid(a_beta.astype(jnp.float32))
    g = -jnp.exp(a_log.astype(jnp.float32)) * jax.nn.softplus(
        a_dec.astype(jnp.float32) + dt_bias.astype(jnp.float32))
    o_a = gated_delta_rule_chunked(gq, gk, gv, g, beta)
    z = a_z.reshape(b_, s_, GDN_HEADS, HEAD_DIM).astype(jnp.float32)
    o_a = (rms_norm(o_a, gdn_norm) * jax.nn.silu(z)).astype(xn.dtype).reshape(b_, s_, D_A)

    o_b = spatial_gating(jax.nn.gelu(b_u), jax.nn.gelu(b_v), sg_ln_g, sg_ln_b, sg_w, sg_b)

    q = apply_rope(c_q.reshape(b_, s_, SA_HEADS, HEAD_DIM), cos_h, sin_h)
    k = apply_rope(c_k, cos_h, sin_h)
    iq = apply_rope(i_q.reshape(b_, s_, IDX_HEADS, IDX_DIM), cos_i, sin_i)
    ik = apply_rope(i_k, cos_i, sin_i)
    iw = i_w * ((IDX_HEADS ** -0.5) * (IDX_DIM ** -0.5))
    o_c = indexer_sparse_attention(q, k, c_v, iq, ik, iw).reshape(b_, s_, D_C)

    return jnp.concatenate([o_a, o_b, o_c], axis=-1) @ w_out


def memory_cross_attention(xn, mn, w_q, w_kv, w_o):
    b_, s_, _ = xn.shape
    m_ = mn.shape[1]
    q = (xn @ w_q).reshape(b_, s_, MEM_HEADS, HEAD_DIM)
    k, v = jnp.split(mn @ w_kv, 2, axis=-1)
    k = k.reshape(b_, m_, MEM_HEADS, HEAD_DIM)
    v = v.reshape(b_, m_, MEM_HEADS, HEAD_DIM)
    logits = jnp.einsum('bshd,bmhd->bhsm', q, k).astype(jnp.float32) * (HEAD_DIM ** -0.5)
    p = jax.nn.softmax(logits, axis=-1).astype(v.dtype)
    o = jnp.einsum('bhsm,bmhd->bshd', p, v).reshape(b_, s_, D_MEM)
    return o @ w_o


def setup_inputs(seed: int = 0) -> dict:
    key = jax.random.key(seed)
    ks = jax.random.split(key, 24)

    def nrm(k, shape, scale):
        return jax.random.normal(k, shape, jnp.float32) * scale

    def gain(k, shape):
        return 1.0 + 0.02 * jax.random.normal(k, shape, jnp.float32)

    dt = jnp.exp(jax.random.uniform(ks[5], (DEPTH, GDN_HEADS), jnp.float32,
                                    minval=float(np.log(1e-3)), maxval=float(np.log(1e-1))))
    return {
        "x": nrm(ks[0], (BATCH, SEQ, D_MODEL), 1.0),
        "mem": nrm(ks[1], (BATCH, MEM_LEN, D_MODEL), 1.0),
        "ln_mix": gain(ks[2], (DEPTH, D_MODEL)),
        "w_in": nrm(ks[3], (DEPTH, D_MODEL, D_IN), D_MODEL ** -0.5),
        "conv_w": nrm(ks[4], (DEPTH, CONV_K, 3 * D_A), CONV_K ** -0.5),
        "a_log": jnp.log(jax.random.uniform(ks[6], (DEPTH, GDN_HEADS), jnp.float32, minval=1.0, maxval=16.0)),
        "dt_bias": dt + jnp.log(-jnp.expm1(-dt)),
        "gdn_norm": gain(ks[7], (DEPTH, HEAD_DIM)),
        "sg_ln_g": gain(ks[8], (DEPTH, D_B)),
        "sg_ln_b": nrm(ks[9], (DEPTH, D_B), 0.02),
        "sg_w": nrm(ks[10], (DEPTH, SG_GROUPS, SG_CHUNK, SG_CHUNK), SG_CHUNK ** -0.5),
        "sg_b": gain(ks[11], (DEPTH, SG_GROUPS, SG_CHUNK)),
        "w_out": nrm(ks[12], (DEPTH, D_MIX, D_MODEL), D_MIX ** -0.5),
        "ln_cross": gain(ks[13], (DEPTH, D_MODEL)),
        "ln_mem": gain(ks[14], (DEPTH, D_MODEL)),
        "w_q_mem": nrm(ks[15], (DEPTH, D_MODEL, D_MEM), D_MODEL ** -0.5),
        "w_kv_mem": nrm(ks[16], (DEPTH, D_MODEL, 2 * D_MEM), D_MODEL ** -0.5),
        "w_o_mem": nrm(ks[17], (DEPTH, D_MEM, D_MODEL), D_MEM ** -0.5),
        "ln_mlp": gain(ks[18], (DEPTH, D_MODEL)),
        "w_up": nrm(ks[19], (DEPTH, D_MODEL, D_FF), D_MODEL ** -0.5),
        "w_down": nrm(ks[20], (DEPTH, D_FF, D_MODEL), D_FF ** -0.5),
        "ln_final": gain(ks[21], (D_MODEL,)),
    }


def reference(x, mem, ln_mix, w_in, conv_w, a_log, dt_bias, gdn_norm, sg_ln_g, sg_ln_b, sg_w, sg_b,
              w_out, ln_cross, ln_mem, w_q_mem, w_kv_mem, w_o_mem, ln_mlp, w_up, w_down, ln_final):
    s_ = x.shape[1]
    cos_h, sin_h = rope_tables(s_, HEAD_DIM)
    cos_i, sin_i = rope_tables(s_, IDX_DIM)
    for l in range(DEPTH):
        h = rms_norm(x, ln_mix[l])
        x = x + hybrid_mixer(h, w_in[l], conv_w[l], a_log[l], dt_bias[l], gdn_norm[l], sg_ln_g[l],
                             sg_ln_b[l], sg_w[l], sg_b[l], w_out[l], cos_h, sin_h, cos_i, sin_i)
        x = x + memory_cross_attention(rms_norm(x, ln_cross[l]), rms_norm(mem, ln_mem[l]),
                                       w_q_mem[l], w_kv_mem[l], w_o_mem[l])
        h = rms_norm(x, ln_mlp[l])
        x = x + jnp.square(jax.nn.relu(h @ w_up[l])) @ w_down[l]
    return rms_norm(x, ln_final)
```

```python
import functools

import jax
import jax.numpy as jnp
import numpy as np
from jax import lax
from jax.experimental import pallas as pl
from jax.experimental.pallas import tpu as pltpu

F32 = jnp.float32
BF16 = jnp.bfloat16

D_MODEL = 2048
HEAD_DIM = 128
GDN_HEADS = 6
GDN_CHUNK = 64
CONV_K = 4
SG_GROUPS = 4
SG_CHUNK = 128
SA_HEADS = 6
IDX_HEADS = 16
IDX_DIM = 64
INDEX_TOPK = 256
MEM_HEADS = 4
D_FF = 4 * D_MODEL
ROPE_THETA = 10000.0
NORM_EPS = 1e-6

D_A = GDN_HEADS * HEAD_DIM
D_B = SG_GROUPS * HEAD_DIM
D_C = SA_HEADS * HEAD_DIM
D_MEM = MEM_HEADS * HEAD_DIM

OFF_IQ = 0
OFF_BU = 1024
OFF_BV = 1536
OFF_CK = 2048
OFF_CV = 2176
OFF_CQ = 2304
OFF_AQKV = 3072
OFF_AZ = 5376
OFF_GATE = 6144
D_IN_PAD = 6272
GATE_IK = 0
GATE_IW = IDX_DIM
GATE_DEC = IDX_DIM + IDX_HEADS
GATE_BETA = GATE_DEC + GDN_HEADS

VMEM_LIMIT = 56 * 1024 * 1024
NEG_INF = float("-inf")


def _cparams(sem):
    return pltpu.CompilerParams(dimension_semantics=sem, vmem_limit_bytes=VMEM_LIMIT)


def _rms_rows(x, g):
    ms = jnp.mean(x * x, axis=-1, keepdims=True)
    return x * lax.rsqrt(ms + NORM_EPS) * g


def _dot(a, b):
    return jnp.dot(a, b, preferred_element_type=F32)


def _dot_nt(a, b, precision=None):
    return lax.dot_general(a, b, (((1,), (1,)), ((), ())), preferred_element_type=F32, precision=precision)


def _dot_tn(a, b):
    return lax.dot_general(a, b, (((0,), (0,)), ((), ())), preferred_element_type=F32)


def _bmm(a, b, precision=None):
    return lax.dot_general(a, b, (((2,), (1,)), ((0,), (0,))), preferred_element_type=F32, precision=precision)


def _bmm_nt(a, b, precision=None):
    return lax.dot_general(a, b, (((2,), (2,)), ((0,), (0,))), preferred_element_type=F32, precision=precision)


def _sigmoid(x):
    return 1.0 / (1.0 + jnp.exp(-x))


def _norm_proj_kernel(x_ref, g_ref, w_ref, o_ref, xn_ref):
    @pl.when(pl.program_id(1) == 0)
    def _():
        xn_ref[...] = _rms_rows(x_ref[...], g_ref[...]).astype(BF16)

    o_ref[...] = _dot(xn_ref[...], w_ref[...]).astype(o_ref.dtype)


def _norm_proj(x, g, w, tm, tn, out_dtype):
    t, d = x.shape
    n = w.shape[1]
    return pl.pallas_call(
        _norm_proj_kernel,
        grid=(t // tm, n // tn),
        in_specs=[pl.BlockSpec((tm, d), lambda i, j: (i, 0)),
                  pl.BlockSpec((1, d), lambda i, j: (0, 0)),
                  pl.BlockSpec((d, tn), lambda i, j: (0, j))],
        out_specs=pl.BlockSpec((tm, tn), lambda i, j: (i, j)),
        out_shape=jax.ShapeDtypeStruct((t, n), out_dtype),
        scratch_shapes=[pltpu.VMEM((tm, d), BF16)],
        compiler_params=_cparams(("parallel", "arbitrary")),
        name="norm_proj",
    )(x, g.reshape(1, d), w)


def _gdn_kernel(alog_ref, dtb_ref, q_ref, k_ref, v_ref, z_ref, gate_ref, cwq_ref, cwk_ref, cwv_ref, gn_ref,
                o_ref, u_s, w_s, qd_s, kd_s, at_s, gl_s, o_s, *, group):
    h = pl.program_id(1)
    s_len = q_ref.shape[0]
    c_len = GDN_CHUNK
    n_chunks = s_len // c_len
    row = lax.broadcasted_iota(jnp.int32, (s_len, 1), 0)

    def conv_silu(x_ref, cw_ref):
        x = x_ref[...]
        cw = cw_ref[...]
        y = x * cw[CONV_K - 1:CONV_K, :]
        for j in range(1, CONV_K):
            xs = jnp.where(row >= j, pltpu.roll(x, j, axis=0), 0.0)
            y = y + xs * cw[CONV_K - 1 - j:CONV_K - j, :]
        return y * _sigmoid(y)

    def l2n(x):
        return x * lax.rsqrt(jnp.sum(x * x, axis=-1, keepdims=True) + NORM_EPS)

    q = l2n(conv_silu(q_ref, cwq_ref)) * (HEAD_DIM ** -0.5)
    k = l2n(conv_silu(k_ref, cwk_ref))
    v = conv_silu(v_ref, cwv_ref)

    gate = gate_ref[...]
    lane = lax.broadcasted_iota(jnp.int32, (1, HEAD_DIM), 1)
    dec = jnp.sum(jnp.where(lane == GATE_DEC + h, gate, 0.0), axis=-1, keepdims=True)
    bet = jnp.sum(jnp.where(lane == GATE_BETA + h, gate, 0.0), axis=-1, keepdims=True)
    beta = jnp.broadcast_to(_sigmoid(bet), (s_len, HEAD_DIM))
    a_neg = -jnp.exp(jnp.full((1, 1), alog_ref[h], F32))
    xg = dec + dtb_ref[h]
    softplus = jnp.maximum(xg, 0.0) + jnp.log(1.0 + jnp.exp(-jnp.abs(xg)))
    g = jnp.broadcast_to(a_neg * softplus, (s_len, HEAD_DIM))

    pos = row % c_len
    gc = g
    shift = 1
    while shift < c_len:
        gc = gc + jnp.where(pos >= shift, pltpu.roll(gc, shift, axis=0), 0.0)
        shift *= 2

    kb = k * beta
    eg = jnp.exp(gc)
    vb = v * beta
    kbe = kb * eg
    qd = q * eg

    ci = lax.broadcasted_iota(jnp.int32, (c_len, c_len), 0)
    cj = lax.broadcasted_iota(jnp.int32, (c_len, c_len), 1)
    tril = (cj <= ci)[None]
    strict = (cj < ci)[None]
    eye = (ci == cj).astype(F32)[None]
    hi = lax.Precision.HIGHEST

    rows_g = group * c_len
    for gi in range(n_chunks // group):
        sl = slice(gi * rows_g, (gi + 1) * rows_g)

        def r3(t):
            return t[sl].reshape(group, c_len, t.shape[-1])

        gc3 = r3(gc)
        k3 = r3(k)
        kb3 = r3(kb)
        lane3 = lax.broadcasted_iota(jnp.int32, (1, 1, HEAD_DIM), 2)
        pmat = jnp.where(lane3 == 0, gc3, jnp.where(lane3 == 1, 1.0, 0.0))
        qmat = jnp.where(lane3 == 0, 1.0, jnp.where(lane3 == 1, -gc3, 0.0))
        dlog = _bmm_nt(pmat, qmat, precision=hi)
        decay = jnp.where(tril, jnp.exp(jnp.where(tril, dlog, 0.0)), 0.0)
        k3b = k3.astype(BF16)
        amat = jnp.where(strict, _bmm_nt(kb3.astype(BF16), k3b) * decay, 0.0)
        attn = _bmm_nt(r3(q).astype(BF16), k3b) * decay
        bpow = -amat
        tinv = eye + bpow
        span = 2
        while span < c_len:
            bpow = _bmm(bpow, bpow, precision=hi)
            tinv = tinv + _bmm(bpow, tinv, precision=hi)
            span *= 2
        rhs = jnp.concatenate([r3(vb), r3(kbe)], axis=-1)
        sol = _bmm(tinv, rhs, precision=hi)
        glast = gc3[:, c_len - 1:c_len, :]
        u_s[sl, :] = sol[:, :, :HEAD_DIM].reshape(rows_g, HEAD_DIM)
        w_s[sl, :] = sol[:, :, HEAD_DIM:].reshape(rows_g, HEAD_DIM).astype(BF16)
        kd_s[sl, :] = (k3 * jnp.exp(glast - gc3)).reshape(rows_g, HEAD_DIM).astype(BF16)
        at_s[sl, :] = attn.reshape(rows_g, c_len).astype(BF16)
        gl_s[sl, :] = jnp.broadcast_to(jnp.exp(glast), (group, c_len, HEAD_DIM)).reshape(rows_g, HEAD_DIM)
    qd_s[...] = qd.astype(BF16)

    def step(n, state):
        r0 = pl.multiple_of(n * c_len, c_len)
        rows = pl.ds(r0, c_len)
        sb = state.astype(BF16)
        v_new = u_s[rows, :] - _dot(w_s[rows, :], sb)
        vnb = v_new.astype(BF16)
        o_s[rows, :] = _dot(qd_s[rows, :], sb) + _dot(at_s[rows, :], vnb)
        gl = jnp.broadcast_to(gl_s[pl.ds(r0, 1), :], (HEAD_DIM, HEAD_DIM))
        return state * gl + _dot_tn(kd_s[rows, :], vnb)

    lax.fori_loop(0, n_chunks, step, jnp.zeros((HEAD_DIM, HEAD_DIM), F32))

    o = o_s[...]
    z = z_ref[...]
    o_ref[...] = (_rms_rows(o, gn_ref[...]) * (z * _sigmoid(z))).astype(o_ref.dtype)


def _gdn(hp, conv_w, a_log, dt_bias, gdn_norm, b_, s_):
    group = min(8, s_ // GDN_CHUNK)
    hb = lambda off, h_mult=1: (lambda b, h: (b, off // HEAD_DIM + h * h_mult))
    cb = lambda off: (lambda b, h: (0, off // HEAD_DIM + h))
    smem = pl.BlockSpec(memory_space=pltpu.SMEM)
    return pl.pallas_call(
        functools.partial(_gdn_kernel, group=group),
        grid=(b_, GDN_HEADS),
        in_specs=[smem, smem,
                  pl.BlockSpec((s_, HEAD_DIM), hb(OFF_AQKV)),
                  pl.BlockSpec((s_, HEAD_DIM), hb(OFF_AQKV + D_A)),
                  pl.BlockSpec((s_, HEAD_DIM), hb(OFF_AQKV + 2 * D_A)),
                  pl.BlockSpec((s_, HEAD_DIM), hb(OFF_AZ)),
                  pl.BlockSpec((s_, HEAD_DIM), hb(OFF_GATE, 0)),
                  pl.BlockSpec((CONV_K, HEAD_DIM), cb(0)),
                  pl.BlockSpec((CONV_K, HEAD_DIM), cb(D_A)),
                  pl.BlockSpec((CONV_K, HEAD_DIM), cb(2 * D_A)),
                  pl.BlockSpec((1, HEAD_DIM), lambda b, h: (0, 0))],
        out_specs=pl.BlockSpec((s_, HEAD_DIM), lambda b, h: (b, h)),
        out_shape=jax.ShapeDtypeStruct((b_ * s_, D_A), BF16),
        scratch_shapes=[pltpu.VMEM((s_, HEAD_DIM), F32),
                        pltpu.VMEM((s_, HEAD_DIM), BF16),
                        pltpu.VMEM((s_, HEAD_DIM), BF16),
                        pltpu.VMEM((s_, HEAD_DIM), BF16),
                        pltpu.VMEM((s_, GDN_CHUNK), BF16),
                        pltpu.VMEM((s_, HEAD_DIM), F32),
                        pltpu.VMEM((s_, HEAD_DIM), F32)],
        compiler_params=_cparams(("parallel", "arbitrary")),
        name="gdn",
    )(a_log, dt_bias, hp, hp, hp, hp, hp, conv_w, conv_w, conv_w, gdn_norm.reshape(1, HEAD_DIM))


def _gelu_tanh(x):
    c = float(np.sqrt(2.0 / np.pi))
    return 0.5 * x * (1.0 + jnp.tanh(c * (x + 0.044715 * (x * x * x))))


def _sg_kernel(u_ref, v_ref, lg_ref, lb_ref, w_ref, bt_ref, o_ref):
    ts = u_ref.shape[0]
    v = _gelu_tanh(v_ref[...])
    mu = jnp.mean(v, axis=-1, keepdims=True)
    vc = v - mu
    var = jnp.mean(vc * vc, axis=-1, keepdims=True)
    vn = (vc * lax.rsqrt(var + NORM_EPS) * lg_ref[...] + lb_ref[...]).astype(BF16)
    ti = lax.broadcasted_iota(jnp.int32, (SG_CHUNK, SG_CHUNK), 0)
    si = lax.broadcasted_iota(jnp.int32, (SG_CHUNK, SG_CHUNK), 1)
    bt = bt_ref[...]
    for g in range(SG_GROUPS):
        wg = jnp.where(si <= ti, w_ref[g], 0.0).astype(BF16)
        bias = jnp.broadcast_to(bt[:, g:g + 1], (SG_CHUNK, HEAD_DIM))
        cs = slice(g * HEAD_DIM, (g + 1) * HEAD_DIM)
        for c in range(ts // SG_CHUNK):
            rs = slice(c * SG_CHUNK, (c + 1) * SG_CHUNK)
            s = _dot(wg, vn[rs, cs]) + bias
            o_ref[rs, cs] = (_gelu_tanh(u_ref[rs, cs]) * s).astype(o_ref.dtype)


def _sg(hp, ln_g, ln_b, w_s, b_s, ts):
    t = hp.shape[0]
    return pl.pallas_call(
        _sg_kernel,
        grid=(t // ts,),
        in_specs=[pl.BlockSpec((ts, D_B), lambda i: (i, OFF_BU // D_B)),
                  pl.BlockSpec((ts, D_B), lambda i: (i, OFF_BV // D_B)),
                  pl.BlockSpec((1, D_B), lambda i: (0, 0)),
                  pl.BlockSpec((1, D_B), lambda i: (0, 0)),
                  pl.BlockSpec((SG_GROUPS, SG_CHUNK, SG_CHUNK), lambda i: (0, 0, 0)),
                  pl.BlockSpec((SG_CHUNK, SG_GROUPS), lambda i: (0, 0))],
        out_specs=pl.BlockSpec((ts, D_B), lambda i: (i, 0)),
        out_shape=jax.ShapeDtypeStruct((t, D_B), BF16),
        compiler_params=_cparams(("parallel",)),
        name="spatial_gating",
    )(hp, hp, ln_g.reshape(1, D_B), ln_b.reshape(1, D_B), w_s, b_s.T)


def _dsa_kernel(iq_ref, q_ref, gq_ref, k_ref, v_ref, gk_ref, cq_ref, sq_ref, ciq_ref, siq_ref,
                ck_ref, sk_ref, cik_ref, sik_ref, o_ref,
                kr_s, vt_s, iklo_s, ikhi_s, sc_s, *, kc, n_sel, n_bisect):
    qi = pl.program_id(1)
    tq = q_ref.shape[0]
    s_len = k_ref.shape[0]
    lane = lax.broadcasted_iota(jnp.int32, (1, HEAD_DIM), 1)

    def rope128(x, cos, sin):
        return x * cos + pltpu.roll(x, HEAD_DIM // 2, axis=1) * sin

    def rope64(x, cos, sin):
        half = IDX_DIM // 2
        partner = jnp.where(lane % IDX_DIM < half, pltpu.roll(x, HEAD_DIM - half, axis=1),
                            pltpu.roll(x, half, axis=1))
        return x * cos + partner * sin

    @pl.when(qi == 0)
    def _():
        kr_s[...] = rope128(k_ref[...], ck_ref[...], sk_ref[...]).astype(BF16)
        for c in range(s_len // kc):
            vt_s[c] = v_ref[c * kc:(c + 1) * kc, :].T.astype(BF16)
        ik = jnp.where(lane < IDX_DIM, rope64(gk_ref[...], cik_ref[...], sik_ref[...]), 0.0)
        iklo_s[...] = ik.astype(BF16)
        ikhi_s[...] = pltpu.roll(ik, IDX_DIM, axis=1).astype(BF16)

    hsel = lax.broadcasted_iota(jnp.int32, (IDX_HEADS, HEAD_DIM), 0)
    lsel = lax.broadcasted_iota(jnp.int32, (IDX_HEADS, HEAD_DIM), 1)
    pick = (lsel == hsel + GATE_IW).astype(F32)
    iw_t = _dot_nt(pick, gq_ref[...], precision=lax.Precision.HIGHEST) * ((IDX_HEADS ** -0.5) * (IDX_DIM ** -0.5))

    n_kc = ((qi + 1) * tq + kc - 1) // kc
    q_pos = qi * tq + lax.broadcasted_iota(jnp.int32, (1, tq), 1)

    iq_pairs = []
    for p in range(IDX_HEADS // 2):
        ls = slice(p * HEAD_DIM, (p + 1) * HEAD_DIM)
        iq_pairs.append(rope64(iq_ref[:, ls], ciq_ref[...], siq_ref[...]).astype(BF16))

    def score_chunk(j, carry):
        smin, smax = carry
        r0 = pl.multiple_of(j * kc, kc)
        rows = pl.ds(r0, kc)
        ik_lo = iklo_s[rows, :]
        ik_hi = ikhi_s[rows, :]
        acc = jnp.zeros((kc, tq), F32)
        for hh in range(IDX_HEADS):
            ikx = ik_lo if hh % 2 == 0 else ik_hi
            rel = jnp.maximum(_dot_nt(ikx, iq_pairs[hh // 2]), 0.0)
            acc = acc + rel * iw_t[hh:hh + 1, :]
        k_pos = r0 + lax.broadcasted_iota(jnp.int32, (kc, 1), 0)
        adm = k_pos <= q_pos
        sc_s[rows, :] = jnp.where(adm, acc, NEG_INF)
        smin = jnp.minimum(smin, jnp.min(jnp.where(adm, acc, jnp.inf), axis=0, keepdims=True))
        smax = jnp.maximum(smax, jnp.max(jnp.where(adm, acc, NEG_INF), axis=0, keepdims=True))
        return smin, smax

    lo, hi = lax.fori_loop(0, n_kc, score_chunk,
                           (jnp.full((1, tq), jnp.inf, F32), jnp.full((1, tq), NEG_INF, F32)))

    def count_ge(thr):
        def body(j, c):
            rows = pl.ds(pl.multiple_of(j * kc, kc), kc)
            return c + jnp.sum(jnp.where(sc_s[rows, :] >= thr, 1.0, 0.0), axis=0, keepdims=True)
        return lax.fori_loop(0, n_kc, body, jnp.zeros((1, tq), F32))

    def bisect(_, carry):
        lo_, hi_ = carry
        mid = 0.5 * (lo_ + hi_)
        ok = count_ge(mid) >= n_sel
        return jnp.where(ok, mid, lo_), jnp.where(ok, hi_, mid)

    lo, hi = lax.fori_loop(0, n_bisect, bisect, (lo, hi))

    scale = HEAD_DIM ** -0.5
    for h in range(SA_HEADS):
        ls = slice(h * HEAD_DIM, (h + 1) * HEAD_DIM)
        qh = rope128(q_ref[:, ls], cq_ref[...], sq_ref[...]).astype(BF16)

        def att_chunk(j, carry):
            m_, l_, acc_ = carry
            r0 = pl.multiple_of(j * kc, kc)
            rows = pl.ds(r0, kc)
            logit = _dot_nt(kr_s[rows, :], qh) * scale
            logit = jnp.where(sc_s[rows, :] >= lo, logit, NEG_INF)
            m_new = jnp.maximum(m_, jnp.max(logit, axis=0, keepdims=True))
            m_use = jnp.where(m_new == NEG_INF, 0.0, m_new)
            alpha = jnp.exp(m_ - m_use)
            p = jnp.exp(logit - m_use)
            l_new = alpha * l_ + jnp.sum(p, axis=0, keepdims=True)
            acc_new = alpha * acc_ + _dot(vt_s[j], p.astype(BF16))
            return m_new, l_new, acc_new

        m, l, acc = lax.fori_loop(0, n_kc, att_chunk,
                                  (jnp.full((1, tq), NEG_INF, F32), jnp.zeros((1, tq), F32),
                                   jnp.zeros((HEAD_DIM, tq), F32)))
        o_ref[:, ls] = (acc / l).T.astype(o_ref.dtype)


def _rope_tables(s_, dim, reps):
    inv_freq = ROPE_THETA ** (-jnp.arange(0, dim, 2, dtype=F32) / dim)
    ang = jnp.arange(s_, dtype=F32)[:, None] * inv_freq[None, :]
    cos, sin = jnp.cos(ang), jnp.sin(ang)
    return (jnp.tile(jnp.concatenate([cos, cos], axis=-1), (1, reps)),
            jnp.tile(jnp.concatenate([-sin, sin], axis=-1), (1, reps)))


def _dsa(hp, b_, s_, tq, kc):
    n_sel = min(INDEX_TOPK, s_ // 4)
    cos_h, sin_h = _rope_tables(s_, HEAD_DIM, 1)
    cos_i, sin_i = _rope_tables(s_, IDX_DIM, 2)
    nq = s_ // tq
    qrow = lambda b, i: b * nq + i
    tab_q = pl.BlockSpec((tq, HEAD_DIM), lambda b, i: (i, 0))
    tab_k = pl.BlockSpec((s_, HEAD_DIM), lambda b, i: (0, 0))
    return pl.pallas_call(
        functools.partial(_dsa_kernel, kc=kc, n_sel=n_sel, n_bisect=40),
        grid=(b_, nq),
        in_specs=[pl.BlockSpec((tq, IDX_HEADS * IDX_DIM), lambda b, i: (qrow(b, i), OFF_IQ // (IDX_HEADS * IDX_DIM))),
                  pl.BlockSpec((tq, D_C), lambda b, i: (qrow(b, i), OFF_CQ // D_C)),
                  pl.BlockSpec((tq, HEAD_DIM), lambda b, i: (qrow(b, i), OFF_GATE // HEAD_DIM)),
                  pl.BlockSpec((s_, HEAD_DIM), lambda b, i: (b, OFF_CK // HEAD_DIM)),
                  pl.BlockSpec((s_, HEAD_DIM), lambda b, i: (b, OFF_CV // HEAD_DIM)),
                  pl.BlockSpec((s_, HEAD_DIM), lambda b, i: (b, OFF_GATE // HEAD_DIM)),
                  tab_q, tab_q, tab_q, tab_q, tab_k, tab_k, tab_k, tab_k],
        out_specs=pl.BlockSpec((tq, D_C), lambda b, i: (qrow(b, i), 0)),
        out_shape=jax.ShapeDtypeStruct((b_ * s_, D_C), BF16),
        scratch_shapes=[pltpu.VMEM((s_, HEAD_DIM), BF16),
                        pltpu.VMEM((s_ // kc, HEAD_DIM, kc), BF16),
                        pltpu.VMEM((s_, HEAD_DIM), BF16),
                        pltpu.VMEM((s_, HEAD_DIM), BF16),
                        pltpu.VMEM((s_, tq), F32)],
        compiler_params=_cparams(("parallel", "arbitrary")),
        name="dsa",
    )(hp, hp, hp, hp, hp, hp, cos_h, sin_h, cos_i, sin_i, cos_h, sin_h, cos_i, sin_i)


def _out_proj_kernel(x_ref, a_ref, b_ref, c_ref, wa_ref, wb_ref, wc_ref, o_ref):
    o_ref[...] = (x_ref[...] + _dot(a_ref[...], wa_ref[...]) + _dot(b_ref[...], wb_ref[...])
                  + _dot(c_ref[...], wc_ref[...]))


def _out_proj(x, oa, ob, oc, w_out, tm):
    t, d = x.shape
    row = lambda w: pl.BlockSpec((tm, w), lambda i: (i, 0))
    full = lambda r: pl.BlockSpec((r, d), lambda i: (0, 0))
    return pl.pallas_call(
        _out_proj_kernel,
        grid=(t // tm,),
        in_specs=[row(d), row(D_A), row(D_B), row(D_C), full(D_A), full(D_B), full(D_C)],
        out_specs=row(d),
        out_shape=jax.ShapeDtypeStruct((t, d), F32),
        compiler_params=_cparams(("parallel",)),
        name="out_proj",
    )(x, oa, ob, oc, w_out[:D_A], w_out[D_A:D_A + D_B], w_out[D_A + D_B:])


def _cross_kernel(x_ref, g_ref, kv_ref, wq_ref, wo_ref, o_ref):
    x = x_ref[...]
    xn = _rms_rows(x, g_ref[...]).astype(BF16)
    q = _dot(xn, wq_ref[...])
    outs = []
    for h in range(MEM_HEADS):
        ls = slice(h * HEAD_DIM, (h + 1) * HEAD_DIM)
        kh = kv_ref[:, ls]
        vh = kv_ref[:, D_MEM + h * HEAD_DIM:D_MEM + (h + 1) * HEAD_DIM]
        logit = _dot_nt(q[:, ls].astype(BF16), kh) * (HEAD_DIM ** -0.5)
        p = jnp.exp(logit - jnp.max(logit, axis=-1, keepdims=True))
        l = jnp.sum(p, axis=-1, keepdims=True)
        outs.append(_dot(p.astype(BF16), vh) / l)
    o = jnp.concatenate(outs, axis=-1).astype(BF16)
    o_ref[...] = x + _dot(o, wo_ref[...])


def _cross(x, g, kv, w_q, w_o, b_, s_, tm):
    d = x.shape[1]
    m_len = kv.shape[0] // b_
    nt = s_ // tm
    return pl.pallas_call(
        _cross_kernel,
        grid=(b_, nt),
        in_specs=[pl.BlockSpec((tm, d), lambda b, i: (b * nt + i, 0)),
                  pl.BlockSpec((1, d), lambda b, i: (0, 0)),
                  pl.BlockSpec((m_len, 2 * D_MEM), lambda b, i: (b, 0)),
                  pl.BlockSpec((d, D_MEM), lambda b, i: (0, 0)),
                  pl.BlockSpec((D_MEM, d), lambda b, i: (0, 0))],
        out_specs=pl.BlockSpec((tm, d), lambda b, i: (b * nt + i, 0)),
        out_shape=jax.ShapeDtypeStruct(x.shape, F32),
        compiler_params=_cparams(("parallel", "parallel")),
        name="cross_attn",
    )(x, g.reshape(1, d), kv, w_q, w_o)


def _mlp_kernel(x_ref, g_ref, wu_ref, wd_ref, o_ref, xn_ref, acc_ref):
    j = pl.program_id(1)

    @pl.when(j == 0)
    def _():
        xn_ref[...] = _rms_rows(x_ref[...], g_ref[...]).astype(BF16)
        acc_ref[...] = jnp.zeros_like(acc_ref)

    hid = jnp.maximum(_dot(xn_ref[...], wu_ref[...]), 0.0)
    acc_ref[...] += _dot((hid * hid).astype(BF16), wd_ref[...])

    @pl.when(j == pl.num_programs(1) - 1)
    def _():
        o_ref[...] = x_ref[...] + acc_ref[...]


def _mlp(x, g, w_up, w_down, tm, tf):
    t, d = x.shape
    f = w_up.shape[1]
    return pl.pallas_call(
        _mlp_kernel,
        grid=(t // tm, f // tf),
        in_specs=[pl.BlockSpec((tm, d), lambda i, j: (i, 0)),
                  pl.BlockSpec((1, d), lambda i, j: (0, 0)),
                  pl.BlockSpec((d, tf), lambda i, j: (0, j)),
                  pl.BlockSpec((tf, d), lambda i, j: (j, 0))],
        out_specs=pl.BlockSpec((tm, d), lambda i, j: (i, 0)),
        out_shape=jax.ShapeDtypeStruct((t, d), F32),
        scratch_shapes=[pltpu.VMEM((tm, d), BF16), pltpu.VMEM((tm, d), F32)],
        compiler_params=_cparams(("parallel", "arbitrary")),
        name="mlp",
    )(x, g.reshape(1, d), w_up, w_down)


def _final_norm_kernel(x_ref, g_ref, o_ref):
    o_ref[...] = _rms_rows(x_ref[...], g_ref[...])


def _final_norm(x, g, tm):
    t, d = x.shape
    return pl.pallas_call(
        _final_norm_kernel,
        grid=(t // tm,),
        in_specs=[pl.BlockSpec((tm, d), lambda i: (i, 0)), pl.BlockSpec((1, d), lambda i: (0, 0))],
        out_specs=pl.BlockSpec((tm, d), lambda i: (i, 0)),
        out_shape=jax.ShapeDtypeStruct((t, d), F32),
        compiler_params=_cparams(("parallel",)),
        name="final_norm",
    )(x, g.reshape(1, d))


def _reorder_w_in(w_in, dtype=BF16):
    widths = (3 * D_A, D_A, GDN_HEADS, GDN_HEADS, D_B, D_B, D_C, HEAD_DIM, HEAD_DIM,
              IDX_HEADS * IDX_DIM, IDX_DIM, IDX_HEADS)
    pts = np.cumsum(widths)[:-1].tolist()
    a_qkv, a_z, a_dec, a_beta, b_u, b_v, c_q, c_k, c_v, i_q, i_k, i_w = jnp.split(w_in, pts, axis=-1)
    pad = jnp.zeros((w_in.shape[0], HEAD_DIM - (IDX_DIM + IDX_HEADS + 2 * GDN_HEADS)), w_in.dtype)
    out = jnp.concatenate([i_q, b_u, b_v, c_k, c_v, c_q, a_qkv, a_z, i_k, i_w, a_dec, a_beta, pad], axis=-1)
    return out.astype(dtype)


def _tile(n, pref):
    t = min(n, pref)
    while n % t:
        t //= 2
    return t


def kernel(x, mem, ln_mix, w_in, conv_w, a_log, dt_bias, gdn_norm, sg_ln_g, sg_ln_b, sg_w, sg_b, w_out, ln_cross,
           ln_mem, w_q_mem, w_kv_mem, w_o_mem, ln_mlp, w_up, w_down, ln_final):
    b_, s_, d = x.shape
    m_len = mem.shape[1]
    depth = w_in.shape[0]
    t = b_ * s_
    tm = _tile(t, 512)
    xf = x.reshape(t, d)
    memf = mem.reshape(b_ * m_len, d)
    for l in range(depth):
        hp = _norm_proj(xf, ln_mix[l], _reorder_w_in(w_in[l]), tm, D_IN_PAD // 7, F32)
        oa = _gdn(hp, conv_w[l], a_log[l], dt_bias[l], gdn_norm[l], b_, s_)
        ob = _sg(hp, sg_ln_g[l], sg_ln_b[l], sg_w[l], sg_b[l], _tile(s_, 512))
        oc = _dsa(hp, b_, s_, _tile(s_, 256), _tile(s_, 256))
        xf = _out_proj(xf, oa, ob, oc, w_out[l].astype(BF16), tm)
        kv = _norm_proj(memf, ln_mem[l], w_kv_mem[l].astype(BF16), _tile(b_ * m_len, 512), 2 * D_MEM, BF16)
        xf = _cross(xf, ln_cross[l], kv, w_q_mem[l].astype(BF16), w_o_mem[l].astype(BF16), b_, s_, _tile(s_, 512))
        xf = _mlp(xf, ln_mlp[l], w_up[l].astype(BF16), w_down[l].astype(BF16), tm, 1024)
    return _final_norm(xf, ln_final, tm).reshape(b_, s_, d)
```

```python
import functools

import jax
import jax.numpy as jnp
import numpy as np
from jax import lax
from jax.experimental import pallas as pl
from jax.experimental.pallas import tpu as pltpu

F32 = jnp.float32
BF16 = jnp.bfloat16

D_MODEL = 2048
HEAD_DIM = 128
GDN_HEADS = 6
GDN_CHUNK = 64
INV_BASE = 8
CONV_K = 4
SG_GROUPS = 4
SG_CHUNK = 128
SA_HEADS = 6
IDX_HEADS = 16
IDX_DIM = 64
INDEX_TOPK = 256
MEM_HEADS = 4
D_FF = 4 * D_MODEL
ROPE_THETA = 10000.0
NORM_EPS = 1e-6

D_A = GDN_HEADS * HEAD_DIM
D_B = SG_GROUPS * HEAD_DIM
D_C = SA_HEADS * HEAD_DIM
D_MEM = MEM_HEADS * HEAD_DIM

OFF_IQ = 0
OFF_BU = 1024
OFF_BV = 1536
OFF_CK = 2048
OFF_CV = 2176
OFF_CQ = 2304
OFF_AQKV = 3072
OFF_AZ = 5376
OFF_GATE = 6144
D_IN_PAD = 6400
GATE_IK = 0
GATE_IW = IDX_DIM
GATE_DEC = IDX_DIM + IDX_HEADS
GATE_BETA = GATE_DEC + GDN_HEADS

VMEM_LIMIT = 56 * 1024 * 1024
NEG_INF = float("-inf")


def _cparams(sem):
    return pltpu.CompilerParams(dimension_semantics=sem, vmem_limit_bytes=VMEM_LIMIT)


def _rms_rows(x, g):
    ms = jnp.mean(x * x, axis=-1, keepdims=True)
    return x * lax.rsqrt(ms + NORM_EPS) * g


def _dot(a, b):
    return jnp.dot(a, b, preferred_element_type=F32)


def _dot_nt(a, b, precision=None):
    return lax.dot_general(a, b, (((1,), (1,)), ((), ())), preferred_element_type=F32, precision=precision)


def _dot_tn(a, b):
    return lax.dot_general(a, b, (((0,), (0,)), ((), ())), preferred_element_type=F32)


def _bmm(a, b, precision=None):
    return lax.dot_general(a, b, (((2,), (1,)), ((0,), (0,))), preferred_element_type=F32, precision=precision)


def _bmm_nt(a, b, precision=None):
    return lax.dot_general(a, b, (((2,), (2,)), ((0,), (0,))), preferred_element_type=F32, precision=precision)


def _sigmoid(x):
    return 1.0 / (1.0 + jnp.exp(-x))


def _norm_proj_kernel(x_ref, g_ref, w_ref, o_ref, xn_ref):
    @pl.when(pl.program_id(1) == 0)
    def _():
        xn_ref[...] = _rms_rows(x_ref[...], g_ref[...]).astype(BF16)

    o_ref[...] = _dot(xn_ref[...], w_ref[...]).astype(o_ref.dtype)


def _norm_proj(x, g, w, tm, tn, out_dtype):
    t, d = x.shape
    n = w.shape[1]
    return pl.pallas_call(
        _norm_proj_kernel,
        grid=(t // tm, n // tn),
        in_specs=[pl.BlockSpec((tm, d), lambda i, j: (i, 0)),
                  pl.BlockSpec((1, d), lambda i, j: (0, 0)),
                  pl.BlockSpec((d, tn), lambda i, j: (0, j))],
        out_specs=pl.BlockSpec((tm, tn), lambda i, j: (i, j)),
        out_shape=jax.ShapeDtypeStruct((t, n), out_dtype),
        scratch_shapes=[pltpu.VMEM((tm, d), BF16)],
        compiler_params=_cparams(("parallel", "arbitrary")),
        name="norm_proj",
    )(x, g.reshape(1, d), w)


def _gdn_kernel(alog_ref, dtb_ref, q_ref, k_ref, v_ref, z_ref, gate_ref, cwq_ref, cwk_ref, cwv_ref, gn_ref,
                o_ref, u_s, w_s, qd_s, kd_s, at_s, gl_s, o_s, *, group, heads):
    s_len = q_ref.shape[0]
    c_len = GDN_CHUNK
    n_chunks = s_len // c_len
    row = lax.broadcasted_iota(jnp.int32, (s_len, 1), 0)
    pos = row % c_len
    lane = lax.broadcasted_iota(jnp.int32, (1, HEAD_DIM), 1)
    lane3 = lax.broadcasted_iota(jnp.int32, (1, 1, HEAD_DIM), 2)
    ci = lax.broadcasted_iota(jnp.int32, (c_len, c_len), 0)
    cj = lax.broadcasted_iota(jnp.int32, (c_len, c_len), 1)
    tril = (cj <= ci)[None]
    strict = (cj < ci)[None]
    base_blk = (ci // INV_BASE == cj // INV_BASE)[None]
    merge_blks = []
    size = INV_BASE
    while size < c_len:
        merge_blks.append(((ci // (2 * size) == cj // (2 * size)) & (ci // size > cj // size))[None])
        size *= 2
    rows_g = group * c_len

    def conv_silu(x, cw):
        y = x * cw[CONV_K - 1:CONV_K, :]
        for j in range(1, CONV_K):
            xs = jnp.where(row >= j, pltpu.roll(x, j, axis=0), 0.0)
            y = y + xs * cw[CONV_K - 1 - j:CONV_K - j, :]
        return y * _sigmoid(y)

    def l2n(x):
        return x * lax.rsqrt(jnp.sum(x * x, axis=-1, keepdims=True) + NORM_EPS)

    gate = gate_ref[...]
    for hl in range(heads):
        h = pl.program_id(1) * heads + hl
        cs = slice(hl * HEAD_DIM, (hl + 1) * HEAD_DIM)
        q = l2n(conv_silu(q_ref[:, cs], cwq_ref[:, cs])) * (HEAD_DIM ** -0.5)
        k = l2n(conv_silu(k_ref[:, cs], cwk_ref[:, cs]))
        v = conv_silu(v_ref[:, cs], cwv_ref[:, cs])

        dec = jnp.sum(jnp.where(lane == GATE_DEC + h, gate, 0.0), axis=-1, keepdims=True)
        bet = jnp.sum(jnp.where(lane == GATE_BETA + h, gate, 0.0), axis=-1, keepdims=True)
        beta = jnp.broadcast_to(_sigmoid(bet), (s_len, HEAD_DIM))
        a_neg = -jnp.exp(jnp.full((1, 1), alog_ref[h], F32))
        xg = dec + dtb_ref[h]
        softplus = jnp.maximum(xg, 0.0) + jnp.log(1.0 + jnp.exp(-jnp.abs(xg)))
        g = jnp.broadcast_to(a_neg * softplus, (s_len, HEAD_DIM))

        gc = g
        shift = 1
        while shift < c_len:
            gc = gc + jnp.where(pos >= shift, pltpu.roll(gc, shift, axis=0), 0.0)
            shift *= 2

        kb = k * beta
        eg = jnp.exp(gc)
        vb = v * beta
        kbe = kb * eg
        qd_s[hl] = (q * eg).astype(BF16)

        for gi in range(n_chunks // group):
            sl = slice(gi * rows_g, (gi + 1) * rows_g)

            def r3(t):
                return t[sl].reshape(group, c_len, t.shape[-1])

            gc3 = r3(gc)
            k3 = r3(k)
            pmat = jnp.where(lane3 == 0, gc3, jnp.where(lane3 == 1, 1.0, 0.0))
            qmat = jnp.where(lane3 == 0, 1.0, jnp.where(lane3 == 1, -gc3, 0.0))
            dlog = _bmm_nt(pmat, qmat, precision=lax.Precision.HIGHEST)
            decay = jnp.where(tril, jnp.exp(jnp.where(tril, dlog, 0.0)), 0.0)
            k3b = k3.astype(BF16)
            amat = jnp.where(strict, _bmm_nt(r3(kb).astype(BF16), k3b) * decay, 0.0)
            attn = _bmm_nt(r3(q).astype(BF16), k3b) * decay
            bpow = jnp.where(base_blk, -amat, 0.0)
            tlow = bpow
            span = 2
            while span < INV_BASE:
                bb = bpow.astype(BF16)
                bpow = _bmm(bb, bb)
                tlow = tlow + bpow + _bmm(bpow.astype(BF16), tlow.astype(BF16))
                span *= 2
            for off_blk in merge_blks:
                a_off = jnp.where(off_blk, amat, 0.0).astype(BF16)
                tb = tlow.astype(BF16)
                xmat = a_off + _bmm(tb, a_off)
                tlow = tlow - xmat - _bmm(xmat.astype(BF16), tb)
            rhs = jnp.concatenate([r3(vb), r3(kbe)], axis=-1)
            sol = rhs + _bmm(tlow.astype(BF16), rhs.astype(BF16))
            glast = gc3[:, c_len - 1:c_len, :]
            u_s[hl, sl, :] = sol[:, :, :HEAD_DIM].reshape(rows_g, HEAD_DIM)
            w_s[hl, sl, :] = sol[:, :, HEAD_DIM:].reshape(rows_g, HEAD_DIM).astype(BF16)
            kd_s[hl, sl, :] = (k3 * jnp.exp(glast - gc3)).reshape(rows_g, HEAD_DIM).astype(BF16)
            at_s[hl, sl, :] = attn.reshape(rows_g, c_len).astype(BF16)
            gl_s[hl, gi * group * 8:(gi + 1) * group * 8, :] = jnp.broadcast_to(
                jnp.exp(glast), (group, 8, HEAD_DIM)).reshape(group * 8, HEAD_DIM)

    def step(n, states):
        rows = pl.ds(pl.multiple_of(n * c_len, c_len), c_len)
        new_states = []
        for hl in range(heads):
            sb = states[hl].astype(BF16)
            v_new = u_s[hl, rows, :] - _dot(w_s[hl, rows, :], sb)
            vnb = v_new.astype(BF16)
            o_s[hl, rows, :] = _dot(qd_s[hl, rows, :], sb) + _dot(at_s[hl, rows, :], vnb)
            gl = jnp.broadcast_to(gl_s[hl, pl.ds(pl.multiple_of(n * 8, 8), 1), :], (HEAD_DIM, HEAD_DIM))
            new_states.append(states[hl] * gl + _dot_tn(kd_s[hl, rows, :], vnb))
        return tuple(new_states)

    lax.fori_loop(0, n_chunks, step, tuple(jnp.zeros((HEAD_DIM, HEAD_DIM), F32) for _ in range(heads)))

    for hl in range(heads):
        cs = slice(hl * HEAD_DIM, (hl + 1) * HEAD_DIM)
        z = z_ref[:, cs]
        o_ref[:, cs] = (_rms_rows(o_s[hl], gn_ref[...]) * (z * _sigmoid(z))).astype(o_ref.dtype)


def _gdn(hp, conv_w, a_log, dt_bias, gdn_norm, b_, s_, heads=2):
    group = min(8, s_ // GDN_CHUNK)
    n_chunks = s_ // GDN_CHUNK
    hw = heads * HEAD_DIM
    hb = lambda off: (lambda b, h: (b, off // hw + h))
    cb = lambda off: (lambda b, h: (0, off // hw + h))
    smem = pl.BlockSpec(memory_space=pltpu.SMEM)
    return pl.pallas_call(
        functools.partial(_gdn_kernel, group=group, heads=heads),
        grid=(b_, GDN_HEADS // heads),
        in_specs=[smem, smem,
                  pl.BlockSpec((s_, hw), hb(OFF_AQKV)),
                  pl.BlockSpec((s_, hw), hb(OFF_AQKV + D_A)),
                  pl.BlockSpec((s_, hw), hb(OFF_AQKV + 2 * D_A)),
                  pl.BlockSpec((s_, hw), hb(OFF_AZ)),
                  pl.BlockSpec((s_, HEAD_DIM), lambda b, h: (b, OFF_GATE // HEAD_DIM)),
                  pl.BlockSpec((CONV_K, hw), cb(0)),
                  pl.BlockSpec((CONV_K, hw), cb(D_A)),
                  pl.BlockSpec((CONV_K, hw), cb(2 * D_A)),
                  pl.BlockSpec((1, HEAD_DIM), lambda b, h: (0, 0))],
        out_specs=pl.BlockSpec((s_, hw), lambda b, h: (b, h)),
        out_shape=jax.ShapeDtypeStruct((b_ * s_, D_A), BF16),
        scratch_shapes=[pltpu.VMEM((heads, s_, HEAD_DIM), F32),
                        pltpu.VMEM((heads, s_, HEAD_DIM), BF16),
                        pltpu.VMEM((heads, s_, HEAD_DIM), BF16),
                        pltpu.VMEM((heads, s_, HEAD_DIM), BF16),
                        pltpu.VMEM((heads, s_, GDN_CHUNK), BF16),
                        pltpu.VMEM((heads, n_chunks * 8, HEAD_DIM), F32),
                        pltpu.VMEM((heads, s_, HEAD_DIM), F32)],
        compiler_params=_cparams(("parallel", "arbitrary")),
        name="gdn",
    )(a_log, dt_bias, hp, hp, hp, hp, hp, conv_w, conv_w, conv_w, gdn_norm.reshape(1, HEAD_DIM))


def _gelu_tanh(x):
    c = float(np.sqrt(2.0 / np.pi))
    return 0.5 * x * (1.0 + jnp.tanh(c * (x + 0.044715 * (x * x * x))))


def _sg_kernel(u_ref, v_ref, lg_ref, lb_ref, w_ref, bt_ref, o_ref):
    ts = u_ref.shape[0]
    v = _gelu_tanh(v_ref[...])
    mu = jnp.mean(v, axis=-1, keepdims=True)
    vc = v - mu
    var = jnp.mean(vc * vc, axis=-1, keepdims=True)
    vn = (vc * lax.rsqrt(var + NORM_EPS) * lg_ref[...] + lb_ref[...]).astype(BF16)
    ti = lax.broadcasted_iota(jnp.int32, (SG_CHUNK, SG_CHUNK), 0)
    si = lax.broadcasted_iota(jnp.int32, (SG_CHUNK, SG_CHUNK), 1)
    bt = bt_ref[...]
    for g in range(SG_GROUPS):
        wg = jnp.where(si <= ti, w_ref[g], 0.0).astype(BF16)
        bias = jnp.broadcast_to(bt[:, g:g + 1], (SG_CHUNK, HEAD_DIM))
        cs = slice(g * HEAD_DIM, (g + 1) * HEAD_DIM)
        for c in range(ts // SG_CHUNK):
            rs = slice(c * SG_CHUNK, (c + 1) * SG_CHUNK)
            s = _dot(wg, vn[rs, cs]) + bias
            o_ref[rs, cs] = (_gelu_tanh(u_ref[rs, cs]) * s).astype(o_ref.dtype)


def _sg(hp, ln_g, ln_b, w_s, b_s, ts):
    t = hp.shape[0]
    return pl.pallas_call(
        _sg_kernel,
        grid=(t // ts,),
        in_specs=[pl.BlockSpec((ts, D_B), lambda i: (i, OFF_BU // D_B)),
                  pl.BlockSpec((ts, D_B), lambda i: (i, OFF_BV // D_B)),
                  pl.BlockSpec((1, D_B), lambda i: (0, 0)),
                  pl.BlockSpec((1, D_B), lambda i: (0, 0)),
                  pl.BlockSpec((SG_GROUPS, SG_CHUNK, SG_CHUNK), lambda i: (0, 0, 0)),
                  pl.BlockSpec((SG_CHUNK, SG_GROUPS), lambda i: (0, 0))],
        out_specs=pl.BlockSpec((ts, D_B), lambda i: (i, 0)),
        out_shape=jax.ShapeDtypeStruct((t, D_B), BF16),
        compiler_params=_cparams(("parallel",)),
        name="spatial_gating",
    )(hp, hp, ln_g.reshape(1, D_B), ln_b.reshape(1, D_B), w_s, b_s.T)


def _dsa_kernel(iq_ref, q_ref, gq_ref, k_ref, v_ref, gk_ref, cq_ref, sq_ref, ciq_ref, siq_ref,
                ck_ref, sk_ref, cik_ref, sik_ref, o_ref,
                kr_s, vt_s, iklo_s, ikhi_s, sc_s, acc_s, *, kc, n_sel, n_bisect):
    qi = pl.program_id(1)
    tq = q_ref.shape[0]
    s_len = k_ref.shape[0]
    lane = lax.broadcasted_iota(jnp.int32, (1, HEAD_DIM), 1)

    def rope128(x, cos, sin):
        return x * cos + pltpu.roll(x, HEAD_DIM // 2, axis=1) * sin

    def rope64(x, cos, sin):
        half = IDX_DIM // 2
        partner = jnp.where(lane % IDX_DIM < half, pltpu.roll(x, HEAD_DIM - half, axis=1),
                            pltpu.roll(x, half, axis=1))
        return x * cos + partner * sin

    @pl.when(qi == 0)
    def _():
        kr_s[...] = rope128(k_ref[...], ck_ref[...], sk_ref[...]).astype(BF16)
        for c in range(s_len // kc):
            vt_s[c] = v_ref[c * kc:(c + 1) * kc, :].T.astype(BF16)
        ik = jnp.where(lane < IDX_DIM, rope64(gk_ref[...], cik_ref[...], sik_ref[...]), 0.0)
        iklo_s[...] = ik.astype(BF16)
        ikhi_s[...] = pltpu.roll(ik, IDX_DIM, axis=1).astype(BF16)

    hsel = lax.broadcasted_iota(jnp.int32, (IDX_HEADS, HEAD_DIM), 0)
    lsel = lax.broadcasted_iota(jnp.int32, (IDX_HEADS, HEAD_DIM), 1)
    pick = (lsel == hsel + GATE_IW).astype(F32)
    iw_t = _dot_nt(pick, gq_ref[...], precision=lax.Precision.HIGHEST) * ((IDX_HEADS ** -0.5) * (IDX_DIM ** -0.5))

    n_kc = ((qi + 1) * tq + kc - 1) // kc
    q_pos = qi * tq + lax.broadcasted_iota(jnp.int32, (1, tq), 1)

    iq_pairs = []
    for p in range(IDX_HEADS // 2):
        ls = slice(p * HEAD_DIM, (p + 1) * HEAD_DIM)
        iq_pairs.append(rope64(iq_ref[:, ls], ciq_ref[...], siq_ref[...]).astype(BF16))

    def score_chunk(j, carry):
        smin, smax = carry
        r0 = pl.multiple_of(j * kc, kc)
        rows = pl.ds(r0, kc)
        ik_lo = iklo_s[rows, :]
        ik_hi = ikhi_s[rows, :]
        acc = jnp.zeros((kc, tq), F32)
        for hh in range(IDX_HEADS):
            ikx = ik_lo if hh % 2 == 0 else ik_hi
            rel = jnp.maximum(_dot_nt(ikx, iq_pairs[hh // 2]), 0.0)
            acc = acc + rel * iw_t[hh:hh + 1, :]
        k_pos = r0 + lax.broadcasted_iota(jnp.int32, (kc, 1), 0)
        adm = k_pos <= q_pos
        sc_s[rows, :] = jnp.where(adm, acc, NEG_INF)
        smin = jnp.minimum(smin, jnp.min(jnp.where(adm, acc, jnp.inf), axis=0, keepdims=True))
        smax = jnp.maximum(smax, jnp.max(jnp.where(adm, acc, NEG_INF), axis=0, keepdims=True))
        return smin, smax

    lo, hi = lax.fori_loop(0, n_kc, score_chunk,
                           (jnp.full((1, tq), jnp.inf, F32), jnp.full((1, tq), NEG_INF, F32)))

    def count_ge(thr):
        def body(j, c):
            rows = pl.ds(pl.multiple_of(j * kc, kc), kc)
            hit = jnp.where(sc_s[rows, :] >= thr, 1.0, 0.0)
            return c + jnp.sum(hit.reshape(kc // 8, 8, tq), axis=0)
        return jnp.sum(lax.fori_loop(0, n_kc, body, jnp.zeros((8, tq), F32)), axis=0, keepdims=True)

    def narrow(state, mid):
        lo_, hi_, c_lo, c_hi = state
        c_mid = count_ge(mid)
        ok = c_mid >= n_sel
        return (jnp.where(ok, mid, lo_), jnp.where(ok, hi_, mid),
                jnp.where(ok, c_mid, c_lo), jnp.where(ok, c_hi, c_mid))

    def search_more(carry):
        it, (_, _, c_lo, _) = carry
        return jnp.logical_and(it < n_bisect, jnp.max(c_lo) > n_sel)

    def search(carry):
        it, state = carry
        lo_, hi_, c_lo, c_hi = state
        frac = (c_lo - (n_sel - 0.5)) / jnp.maximum(c_lo - c_hi, 1.0)
        state = narrow(state, lo_ + (hi_ - lo_) * jnp.clip(frac, 0.0, 1.0))
        state = narrow(state, 0.5 * (state[0] + state[1]))
        return it + 1, state

    n_adm = (q_pos + 1).astype(F32)
    _, (lo, hi, _, _) = lax.while_loop(search_more, search,
                                       (jnp.int32(0), (lo, hi, n_adm, jnp.zeros((1, tq), F32))))

    qscale = (HEAD_DIM ** -0.5) * float(np.log2(np.e))
    q_all = jnp.concatenate(
        [(rope128(q_ref[:, h * HEAD_DIM:(h + 1) * HEAD_DIM], cq_ref[...], sq_ref[...]) * qscale).astype(BF16)
         for h in range(SA_HEADS)], axis=0)
    acc_s[...] = jnp.zeros_like(acc_s)

    def att_chunk(j, carry):
        m_, l_ = carry
        rows = pl.ds(pl.multiple_of(j * kc, kc), kc)
        bias = jnp.where(sc_s[rows, :] >= lo, 0.0, NEG_INF)
        logit = _dot_nt(kr_s[rows, :], q_all) + jnp.concatenate([bias] * SA_HEADS, axis=1)
        m_new = jnp.maximum(m_, jnp.max(logit, axis=0, keepdims=True))
        m_use = jnp.where(m_new == NEG_INF, 0.0, m_new)
        alpha = jnp.exp2(m_ - m_use)
        p = jnp.exp2(logit - m_use)
        acc_s[...] = alpha * acc_s[...] + _dot(vt_s[j], p.astype(BF16))
        return m_new, alpha * l_ + jnp.sum(p, axis=0, keepdims=True)

    _, l = lax.fori_loop(0, n_kc, att_chunk,
                         (jnp.full((1, SA_HEADS * tq), NEG_INF, F32), jnp.zeros((1, SA_HEADS * tq), F32)))
    out_t = acc_s[...] / l
    for h in range(SA_HEADS):
        o_ref[:, h * HEAD_DIM:(h + 1) * HEAD_DIM] = out_t[:, h * tq:(h + 1) * tq].T.astype(o_ref.dtype)


def _rope_tables(s_, dim, reps):
    inv_freq = ROPE_THETA ** (-jnp.arange(0, dim, 2, dtype=F32) / dim)
    ang = jnp.arange(s_, dtype=F32)[:, None] * inv_freq[None, :]
    cos, sin = jnp.cos(ang), jnp.sin(ang)
    return (jnp.tile(jnp.concatenate([cos, cos], axis=-1), (1, reps)),
            jnp.tile(jnp.concatenate([-sin, sin], axis=-1), (1, reps)))


def _dsa(hp, b_, s_, tq, kc):
    n_sel = min(INDEX_TOPK, s_ // 4)
    cos_h, sin_h = _rope_tables(s_, HEAD_DIM, 1)
    cos_i, sin_i = _rope_tables(s_, IDX_DIM, 2)
    nq = s_ // tq
    qrow = lambda b, i: b * nq + i
    tab_q = pl.BlockSpec((tq, HEAD_DIM), lambda b, i: (i, 0))
    tab_k = pl.BlockSpec((s_, HEAD_DIM), lambda b, i: (0, 0))
    return pl.pallas_call(
        functools.partial(_dsa_kernel, kc=kc, n_sel=n_sel, n_bisect=40),
        grid=(b_, nq),
        in_specs=[pl.BlockSpec((tq, IDX_HEADS * IDX_DIM), lambda b, i: (qrow(b, i), OFF_IQ // (IDX_HEADS * IDX_DIM))),
                  pl.BlockSpec((tq, D_C), lambda b, i: (qrow(b, i), OFF_CQ // D_C)),
                  pl.BlockSpec((tq, HEAD_DIM), lambda b, i: (qrow(b, i), OFF_GATE // HEAD_DIM)),
                  pl.BlockSpec((s_, HEAD_DIM), lambda b, i: (b, OFF_CK // HEAD_DIM)),
                  pl.BlockSpec((s_, HEAD_DIM), lambda b, i: (b, OFF_CV // HEAD_DIM)),
                  pl.BlockSpec((s_, HEAD_DIM), lambda b, i: (b, OFF_GATE // HEAD_DIM)),
                  tab_q, tab_q, tab_q, tab_q, tab_k, tab_k, tab_k, tab_k],
        out_specs=pl.BlockSpec((tq, D_C), lambda b, i: (qrow(b, i), 0)),
        out_shape=jax.ShapeDtypeStruct((b_ * s_, D_C), BF16),
        scratch_shapes=[pltpu.VMEM((s_, HEAD_DIM), BF16),
                        pltpu.VMEM((s_ // kc, HEAD_DIM, kc), BF16),
                        pltpu.VMEM((s_, HEAD_DIM), BF16),
                        pltpu.VMEM((s_, HEAD_DIM), BF16),
                        pltpu.VMEM((s_, tq), F32),
                        pltpu.VMEM((HEAD_DIM, SA_HEADS * tq), F32)],
        compiler_params=_cparams(("parallel", "arbitrary")),
        name="dsa",
    )(hp, hp, hp, hp, hp, hp, cos_h, sin_h, cos_i, sin_i, cos_h, sin_h, cos_i, sin_i)


def _out_proj_kernel(x_ref, a_ref, b_ref, c_ref, wa_ref, wb_ref, wc_ref, o_ref):
    o_ref[...] = (x_ref[...] + _dot(a_ref[...], wa_ref[...]) + _dot(b_ref[...], wb_ref[...])
                  + _dot(c_ref[...], wc_ref[...]))


def _out_proj(x, oa, ob, oc, w_out, tm):
    t, d = x.shape
    row = lambda w: pl.BlockSpec((tm, w), lambda i: (i, 0))
    full = lambda r: pl.BlockSpec((r, d), lambda i: (0, 0))
    return pl.pallas_call(
        _out_proj_kernel,
        grid=(t // tm,),
        in_specs=[row(d), row(D_A), row(D_B), row(D_C), full(D_A), full(D_B), full(D_C)],
        out_specs=row(d),
        out_shape=jax.ShapeDtypeStruct((t, d), F32),
        compiler_params=_cparams(("parallel",)),
        name="out_proj",
    )(x, oa, ob, oc, w_out[:D_A], w_out[D_A:D_A + D_B], w_out[D_A + D_B:])


def _cross_kernel(x_ref, g_ref, kv_ref, wq_ref, wo_ref, o_ref):
    x = x_ref[...]
    xn = _rms_rows(x, g_ref[...]).astype(BF16)
    q = _dot(xn, wq_ref[...])
    outs = []
    for h in range(MEM_HEADS):
        ls = slice(h * HEAD_DIM, (h + 1) * HEAD_DIM)
        kh = kv_ref[:, ls]
        vh = kv_ref[:, D_MEM + h * HEAD_DIM:D_MEM + (h + 1) * HEAD_DIM]
        logit = _dot_nt(q[:, ls].astype(BF16), kh) * (HEAD_DIM ** -0.5)
        p = jnp.exp(logit - jnp.max(logit, axis=-1, keepdims=True))
        l = jnp.sum(p, axis=-1, keepdims=True)
        outs.append(_dot(p.astype(BF16), vh) / l)
    o = jnp.concatenate(outs, axis=-1).astype(BF16)
    o_ref[...] = x + _dot(o, wo_ref[...])


def _cross(x, g, kv, w_q, w_o, b_, s_, tm):
    d = x.shape[1]
    m_len = kv.shape[0] // b_
    nt = s_ // tm
    return pl.pallas_call(
        _cross_kernel,
        grid=(b_, nt),
        in_specs=[pl.BlockSpec((tm, d), lambda b, i: (b * nt + i, 0)),
                  pl.BlockSpec((1, d), lambda b, i: (0, 0)),
                  pl.BlockSpec((m_len, 2 * D_MEM), lambda b, i: (b, 0)),
                  pl.BlockSpec((d, D_MEM), lambda b, i: (0, 0)),
                  pl.BlockSpec((D_MEM, d), lambda b, i: (0, 0))],
        out_specs=pl.BlockSpec((tm, d), lambda b, i: (b * nt + i, 0)),
        out_shape=jax.ShapeDtypeStruct(x.shape, F32),
        compiler_params=_cparams(("parallel", "parallel")),
        name="cross_attn",
    )(x, g.reshape(1, d), kv, w_q, w_o)


def _mlp_kernel(x_ref, g_ref, wu_ref, wd_ref, o_ref, xn_ref, acc_ref):
    j = pl.program_id(1)

    @pl.when(j == 0)
    def _():
        xn_ref[...] = _rms_rows(x_ref[...], g_ref[...]).astype(BF16)
        acc_ref[...] = jnp.zeros_like(acc_ref)

    hid = jnp.maximum(_dot(xn_ref[...], wu_ref[...]), 0.0)
    acc_ref[...] += _dot((hid * hid).astype(BF16), wd_ref[...])

    @pl.when(j == pl.num_programs(1) - 1)
    def _():
        o_ref[...] = x_ref[...] + acc_ref[...]


def _mlp(x, g, w_up, w_down, tm, tf):
    t, d = x.shape
    f = w_up.shape[1]
    return pl.pallas_call(
        _mlp_kernel,
        grid=(t // tm, f // tf),
        in_specs=[pl.BlockSpec((tm, d), lambda i, j: (i, 0)),
                  pl.BlockSpec((1, d), lambda i, j: (0, 0)),
                  pl.BlockSpec((d, tf), lambda i, j: (0, j)),
                  pl.BlockSpec((tf, d), lambda i, j: (j, 0))],
        out_specs=pl.BlockSpec((tm, d), lambda i, j: (i, 0)),
        out_shape=jax.ShapeDtypeStruct((t, d), F32),
        scratch_shapes=[pltpu.VMEM((tm, d), BF16), pltpu.VMEM((tm, d), F32)],
        compiler_params=_cparams(("parallel", "arbitrary")),
        name="mlp",
    )(x, g.reshape(1, d), w_up, w_down)


def _final_norm_kernel(x_ref, g_ref, o_ref):
    o_ref[...] = _rms_rows(x_ref[...], g_ref[...])


def _final_norm(x, g, tm):
    t, d = x.shape
    return pl.pallas_call(
        _final_norm_kernel,
        grid=(t // tm,),
        in_specs=[pl.BlockSpec((tm, d), lambda i: (i, 0)), pl.BlockSpec((1, d), lambda i: (0, 0))],
        out_specs=pl.BlockSpec((tm, d), lambda i: (i, 0)),
        out_shape=jax.ShapeDtypeStruct((t, d), F32),
        compiler_params=_cparams(("parallel",)),
        name="final_norm",
    )(x, g.reshape(1, d))


def _w_in_moves():
    widths = (3 * D_A, D_A, GDN_HEADS, GDN_HEADS, D_B, D_B, D_C, HEAD_DIM, HEAD_DIM,
              IDX_HEADS * IDX_DIM, IDX_DIM, IDX_HEADS)
    a_qkv, a_z, a_dec, _, b_u, b_v, c_q, c_k, c_v, i_q, i_k, _ = (int(v) for v in np.cumsum((0,) + widths)[:-1])
    return ((i_q, IDX_HEADS * IDX_DIM, OFF_IQ), (b_u, D_B, OFF_BU), (b_v, D_B, OFF_BV), (c_k, HEAD_DIM, OFF_CK),
            (c_v, HEAD_DIM, OFF_CV), (c_q, D_C, OFF_CQ), (a_qkv, 3 * D_A, OFF_AQKV), (a_z, D_A, OFF_AZ),
            (i_k, IDX_DIM + IDX_HEADS, OFF_GATE + GATE_IK), (a_dec, 2 * GDN_HEADS, OFF_GATE + GATE_DEC))


def _prep_w_in_kernel(w_ref, o_ref):
    o_ref[:, OFF_GATE:] = jnp.zeros((o_ref.shape[0], D_IN_PAD - OFF_GATE), o_ref.dtype)
    for src, width, dst in _w_in_moves():
        o_ref[:, dst:dst + width] = w_ref[:, src:src + width].astype(o_ref.dtype)


def _prep_w_in(w_in, dtype=BF16):
    depth, d, n = w_in.shape
    rt = _tile(d, 256)
    return pl.pallas_call(
        _prep_w_in_kernel,
        grid=(depth, d // rt),
        in_specs=[pl.BlockSpec((None, rt, n), lambda l, i: (l, i, 0))],
        out_specs=pl.BlockSpec((None, rt, D_IN_PAD), lambda l, i: (l, i, 0)),
        out_shape=jax.ShapeDtypeStruct((depth, d, D_IN_PAD), dtype),
        compiler_params=_cparams(("parallel", "parallel")),
        name="prep_w_in",
    )(w_in)


def _tile(n, pref):
    t = min(n, pref)
    while n % t:
        t //= 2
    return t


def kernel(x, mem, ln_mix, w_in, conv_w, a_log, dt_bias, gdn_norm, sg_ln_g, sg_ln_b, sg_w, sg_b, w_out, ln_cross,
           ln_mem, w_q_mem, w_kv_mem, w_o_mem, ln_mlp, w_up, w_down, ln_final):
    b_, s_, d = x.shape
    m_len = mem.shape[1]
    depth = w_in.shape[0]
    t = b_ * s_
    tm = _tile(t, 512)
    xf = x.reshape(t, d)
    memf = mem.reshape(b_ * m_len, d)
    w_in_p = _prep_w_in(w_in)
    for l in range(depth):
        hp = _norm_proj(xf, ln_mix[l], w_in_p[l], _tile(t, 1024), D_IN_PAD // 5, F32)
        oa = _gdn(hp, conv_w[l], a_log[l], dt_bias[l], gdn_norm[l], b_, s_)
        ob = _sg(hp, sg_ln_g[l], sg_ln_b[l], sg_w[l], sg_b[l], _tile(s_, 512))
        oc = _dsa(hp, b_, s_, _tile(s_, 256), _tile(s_, 256))
        xf = _out_proj(xf, oa, ob, oc, w_out[l].astype(BF16), tm)
        kv = _norm_proj(memf, ln_mem[l], w_kv_mem[l].astype(BF16), _tile(b_ * m_len, 512), 2 * D_MEM, BF16)
        xf = _cross(xf, ln_cross[l], kv, w_q_mem[l].astype(BF16), w_o_mem[l].astype(BF16), b_, s_, _tile(s_, 512))
        xf = _mlp(xf, ln_mlp[l], w_up[l].astype(BF16), w_down[l].astype(BF16), tm, 1024)
    return _final_norm(xf, ln_final, tm).reshape(b_, s_, d)
```

```python
import functools

import jax
import jax.numpy as jnp
import numpy as np
from jax import lax
from jax.experimental import pallas as pl
from jax.experimental.pallas import tpu as pltpu

F32 = jnp.float32
BF16 = jnp.bfloat16

D_MODEL = 2048
HEAD_DIM = 128
GDN_HEADS = 6
GDN_CHUNK = 128
INV_BASE = 8
GDN_GROUP_ROWS = 512
CONV_K = 4
CONV_PAD = 8
SG_GROUPS = 4
SG_CHUNK = 128
SA_HEADS = 6
IDX_HEADS = 16
IDX_DIM = 64
INDEX_TOPK = 256
MEM_HEADS = 4
D_FF = 4 * D_MODEL
ROPE_THETA = 10000.0
NORM_EPS = 1e-6

D_A = GDN_HEADS * HEAD_DIM
D_B = SG_GROUPS * HEAD_DIM
D_C = SA_HEADS * HEAD_DIM
D_MEM = MEM_HEADS * HEAD_DIM

OFF_IQ = 0
OFF_BU = 1024
OFF_BV = 1536
OFF_CK = 2048
OFF_CV = 2176
OFF_CQ = 2304
OFF_AQKV = 3072
OFF_AZ = 5376
OFF_GATE = 6144
D_IN_PAD = 6400
GATE_IK = 0
GATE_IW = IDX_DIM
GATE_DEC = IDX_DIM + IDX_HEADS
GATE_BETA = GATE_DEC + GDN_HEADS

VMEM_LIMIT = 56 * 1024 * 1024
NEG_INF = float("-inf")


def _cparams(sem):
    return pltpu.CompilerParams(dimension_semantics=sem, vmem_limit_bytes=VMEM_LIMIT)


def _rms_rows(x, g):
    ms = jnp.mean(x * x, axis=-1, keepdims=True)
    return x * lax.rsqrt(ms + NORM_EPS) * g


def _dot(a, b):
    return jnp.dot(a, b, preferred_element_type=F32)


def _dot_nt(a, b, precision=None):
    return lax.dot_general(a, b, (((1,), (1,)), ((), ())), preferred_element_type=F32, precision=precision)


def _dot_tn(a, b):
    return lax.dot_general(a, b, (((0,), (0,)), ((), ())), preferred_element_type=F32)


def _bmm(a, b, precision=None):
    return lax.dot_general(a, b, (((2,), (1,)), ((0,), (0,))), preferred_element_type=F32, precision=precision)


def _bmm_nt(a, b, precision=None):
    return lax.dot_general(a, b, (((2,), (2,)), ((0,), (0,))), preferred_element_type=F32, precision=precision)


def _sigmoid(x):
    return 1.0 / (1.0 + jnp.exp(-x))


def _norm_proj_kernel(x_ref, g_ref, w_ref, o_ref, xn_ref):
    @pl.when(pl.program_id(1) == 0)
    def _():
        xn_ref[...] = _rms_rows(x_ref[...], g_ref[...]).astype(BF16)

    o_ref[...] = _dot(xn_ref[...], w_ref[...]).astype(o_ref.dtype)


def _norm_proj(x, g, w, l, tm, tn, out_dtype):
    t, d = x.shape
    n = w.shape[2]
    return pl.pallas_call(
        _norm_proj_kernel,
        grid=(t // tm, n // tn),
        in_specs=[pl.BlockSpec((tm, d), lambda i, j: (i, 0)),
                  pl.BlockSpec((1, d), lambda i, j: (0, 0)),
                  pl.BlockSpec((None, d, tn), lambda i, j: (l, 0, j))],
        out_specs=pl.BlockSpec((tm, tn), lambda i, j: (i, j)),
        out_shape=jax.ShapeDtypeStruct((t, n), out_dtype),
        scratch_shapes=[pltpu.VMEM((tm, d), BF16)],
        compiler_params=_cparams(("parallel", "arbitrary")),
        name="norm_proj",
    )(x, g.reshape(1, d), w)


def _gdn_kernel(alog_ref, dtb_ref, q_ref, k_ref, v_ref, z_ref, gate_ref, cwq_ref, cwk_ref, cwv_ref, gn_ref,
                o_ref, u_s, w_s, qd_s, kd_s, at_s, gl_s, o_s, pad_s, *, group, heads):
    s_len = q_ref.shape[0]
    c_len = GDN_CHUNK
    n_chunks = s_len // c_len
    row = lax.broadcasted_iota(jnp.int32, (s_len, 1), 0)
    pos = row % c_len
    lane = lax.broadcasted_iota(jnp.int32, (1, HEAD_DIM), 1)
    lane3 = lax.broadcasted_iota(jnp.int32, (1, 1, HEAD_DIM), 2)
    ci = lax.broadcasted_iota(jnp.int32, (c_len, c_len), 0)
    cj = lax.broadcasted_iota(jnp.int32, (c_len, c_len), 1)
    tril = (cj <= ci)[None]
    strict = (cj < ci)[None]
    base_blk = (ci // INV_BASE == cj // INV_BASE)[None]
    merge_blks = []
    size = INV_BASE
    while size < c_len:
        merge_blks.append(((ci // (2 * size) == cj // (2 * size)) & (ci // size > cj // size))[None])
        size *= 2
    rows_g = group * c_len

    pad_s[0:CONV_PAD, :] = jnp.zeros((CONV_PAD, HEAD_DIM), F32)

    def conv_silu(x, cw):
        pad_s[CONV_PAD:, :] = x
        y = x * cw[CONV_K - 1:CONV_K, :]
        for j in range(1, CONV_K):
            y = y + pad_s[CONV_PAD - j:CONV_PAD - j + s_len, :] * cw[CONV_K - 1 - j:CONV_K - j, :]
        return y * _sigmoid(y)

    def l2n(x):
        return x * lax.rsqrt(jnp.sum(x * x, axis=-1, keepdims=True) + NORM_EPS)

    gate = gate_ref[...]
    for hl in range(heads):
        h = pl.program_id(1) * heads + hl
        cs = slice(hl * HEAD_DIM, (hl + 1) * HEAD_DIM)
        q = l2n(conv_silu(q_ref[:, cs], cwq_ref[:, cs])) * (HEAD_DIM ** -0.5)
        k = l2n(conv_silu(k_ref[:, cs], cwk_ref[:, cs]))
        v = conv_silu(v_ref[:, cs], cwv_ref[:, cs])

        dec = jnp.sum(jnp.where(lane == GATE_DEC + h, gate, 0.0), axis=-1, keepdims=True)
        bet = jnp.sum(jnp.where(lane == GATE_BETA + h, gate, 0.0), axis=-1, keepdims=True)
        beta = jnp.broadcast_to(_sigmoid(bet), (s_len, HEAD_DIM))
        a_neg = -jnp.exp(jnp.full((1, 1), alog_ref[h], F32))
        xg = dec + dtb_ref[h]
        softplus = jnp.maximum(xg, 0.0) + jnp.log(1.0 + jnp.exp(-jnp.abs(xg)))
        g = jnp.broadcast_to(a_neg * softplus, (s_len, HEAD_DIM))

        gc = g
        shift = 1
        while shift < c_len:
            gc = gc + jnp.where(pos >= shift, pltpu.roll(gc, shift, axis=0), 0.0)
            shift *= 2

        kb = k * beta
        eg = jnp.exp(gc)
        vb = v * beta
        kbe = kb * eg
        qd_s[hl] = (q * eg).astype(BF16)

        for gi in range(n_chunks // group):
            sl = slice(gi * rows_g, (gi + 1) * rows_g)

            def r3(t):
                return t[sl].reshape(group, c_len, t.shape[-1])

            gc3 = r3(gc)
            k3 = r3(k)
            pmat = jnp.where(lane3 == 0, gc3, jnp.where(lane3 == 1, 1.0, 0.0))
            qmat = jnp.where(lane3 == 0, 1.0, jnp.where(lane3 == 1, -gc3, 0.0))
            dlog = _bmm_nt(pmat, qmat, precision=lax.Precision.HIGHEST)
            decay = jnp.where(tril, jnp.exp(jnp.where(tril, dlog, 0.0)), 0.0)
            k3b = k3.astype(BF16)
            amat = jnp.where(strict, _bmm_nt(r3(kb).astype(BF16), k3b) * decay, 0.0)
            attn = _bmm_nt(r3(q).astype(BF16), k3b) * decay
            bpow = jnp.where(base_blk, -amat, 0.0)
            tlow = bpow
            span = 2
            while span < INV_BASE:
                bb = bpow.astype(BF16)
                bpow = _bmm(bb, bb)
                tlow = tlow + bpow + _bmm(bpow.astype(BF16), tlow.astype(BF16))
                span *= 2
            for off_blk in merge_blks:
                a_off = jnp.where(off_blk, amat, 0.0).astype(BF16)
                tb = tlow.astype(BF16)
                xmat = a_off + _bmm(tb, a_off)
                tlow = tlow - xmat - _bmm(xmat.astype(BF16), tb)
            rhs = jnp.concatenate([r3(vb), r3(kbe)], axis=-1)
            sol = rhs + _bmm(tlow.astype(BF16), rhs.astype(BF16))
            glast = gc3[:, c_len - 1:c_len, :]
            u_s[hl, sl, :] = sol[:, :, :HEAD_DIM].reshape(rows_g, HEAD_DIM)
            w_s[hl, sl, :] = sol[:, :, HEAD_DIM:].reshape(rows_g, HEAD_DIM).astype(BF16)
            kd_s[hl, sl, :] = (k3 * jnp.exp(glast - gc3)).reshape(rows_g, HEAD_DIM).astype(BF16)
            at_s[hl, sl, :] = attn.reshape(rows_g, c_len).astype(BF16)
            gl_s[hl, gi * group * 8:(gi + 1) * group * 8, :] = jnp.broadcast_to(
                jnp.exp(glast), (group, 8, HEAD_DIM)).reshape(group * 8, HEAD_DIM)

    def step(n, states):
        rows = pl.ds(pl.multiple_of(n * c_len, c_len), c_len)
        new_states = []
        for hl in range(heads):
            sb = states[hl].astype(BF16)
            v_new = u_s[hl, rows, :] - _dot(w_s[hl, rows, :], sb)
            vnb = v_new.astype(BF16)
            o_s[hl, rows, :] = _dot(qd_s[hl, rows, :], sb) + _dot(at_s[hl, rows, :], vnb)
            gl = jnp.broadcast_to(gl_s[hl, pl.ds(pl.multiple_of(n * 8, 8), 1), :], (HEAD_DIM, HEAD_DIM))
            new_states.append(states[hl] * gl + _dot_tn(kd_s[hl, rows, :], vnb))
        return tuple(new_states)

    lax.fori_loop(0, n_chunks, step, tuple(jnp.zeros((HEAD_DIM, HEAD_DIM), F32) for _ in range(heads)))

    for hl in range(heads):
        cs = slice(hl * HEAD_DIM, (hl + 1) * HEAD_DIM)
        z = z_ref[:, cs]
        o_ref[:, cs] = (_rms_rows(o_s[hl], gn_ref[...]) * (z * _sigmoid(z))).astype(o_ref.dtype)


def _gdn(hp, conv_w, a_log, dt_bias, gdn_norm, b_, s_, heads=2):
    n_chunks = s_ // GDN_CHUNK
    group = min(GDN_GROUP_ROWS // GDN_CHUNK, n_chunks)
    hw = heads * HEAD_DIM
    hb = lambda off: (lambda b, h: (b, off // hw + h))
    cb = lambda off: (lambda b, h: (0, off // hw + h))
    smem = pl.BlockSpec(memory_space=pltpu.SMEM)
    return pl.pallas_call(
        functools.partial(_gdn_kernel, group=group, heads=heads),
        grid=(b_, GDN_HEADS // heads),
        in_specs=[smem, smem,
                  pl.BlockSpec((s_, hw), hb(OFF_AQKV)),
                  pl.BlockSpec((s_, hw), hb(OFF_AQKV + D_A)),
                  pl.BlockSpec((s_, hw), hb(OFF_AQKV + 2 * D_A)),
                  pl.BlockSpec((s_, hw), hb(OFF_AZ)),
                  pl.BlockSpec((s_, HEAD_DIM), lambda b, h: (b, OFF_GATE // HEAD_DIM)),
                  pl.BlockSpec((CONV_K, hw), cb(0)),
                  pl.BlockSpec((CONV_K, hw), cb(D_A)),
                  pl.BlockSpec((CONV_K, hw), cb(2 * D_A)),
                  pl.BlockSpec((1, HEAD_DIM), lambda b, h: (0, 0))],
        out_specs=pl.BlockSpec((s_, hw), lambda b, h: (b, h)),
        out_shape=jax.ShapeDtypeStruct((b_ * s_, D_A), BF16),
        scratch_shapes=[pltpu.VMEM((heads, s_, HEAD_DIM), F32),
                        pltpu.VMEM((heads, s_, HEAD_DIM), BF16),
                        pltpu.VMEM((heads, s_, HEAD_DIM), BF16),
                        pltpu.VMEM((heads, s_, HEAD_DIM), BF16),
                        pltpu.VMEM((heads, s_, GDN_CHUNK), BF16),
                        pltpu.VMEM((heads, n_chunks * 8, HEAD_DIM), F32),
                        pltpu.VMEM((heads, s_, HEAD_DIM), F32),
                        pltpu.VMEM((CONV_PAD + s_, HEAD_DIM), F32)],
        compiler_params=_cparams(("parallel", "arbitrary")),
        name="gdn",
    )(a_log, dt_bias, hp, hp, hp, hp, hp, conv_w, conv_w, conv_w, gdn_norm.reshape(1, HEAD_DIM))


def _gelu_tanh(x):
    c = float(np.sqrt(2.0 / np.pi))
    return 0.5 * x * (1.0 + jnp.tanh(c * (x + 0.044715 * (x * x * x))))


def _sg_kernel(u_ref, v_ref, lg_ref, lb_ref, w_ref, bt_ref, o_ref):
    ts = u_ref.shape[0]
    v = _gelu_tanh(v_ref[...])
    mu = jnp.mean(v, axis=-1, keepdims=True)
    vc = v - mu
    var = jnp.mean(vc * vc, axis=-1, keepdims=True)
    vn = (vc * lax.rsqrt(var + NORM_EPS) * lg_ref[...] + lb_ref[...]).astype(BF16)
    ti = lax.broadcasted_iota(jnp.int32, (SG_CHUNK, SG_CHUNK), 0)
    si = lax.broadcasted_iota(jnp.int32, (SG_CHUNK, SG_CHUNK), 1)
    bt = bt_ref[...]
    for g in range(SG_GROUPS):
        wg = jnp.where(si <= ti, w_ref[g], 0.0).astype(BF16)
        bias = jnp.broadcast_to(bt[:, g:g + 1], (SG_CHUNK, HEAD_DIM))
        cs = slice(g * HEAD_DIM, (g + 1) * HEAD_DIM)
        for c in range(ts // SG_CHUNK):
            rs = slice(c * SG_CHUNK, (c + 1) * SG_CHUNK)
            s = _dot(wg, vn[rs, cs]) + bias
            o_ref[rs, cs] = (_gelu_tanh(u_ref[rs, cs]) * s).astype(o_ref.dtype)


def _sg(hp, ln_g, ln_b, w_s, b_s, ts):
    t = hp.shape[0]
    return pl.pallas_call(
        _sg_kernel,
        grid=(t // ts,),
        in_specs=[pl.BlockSpec((ts, D_B), lambda i: (i, OFF_BU // D_B)),
                  pl.BlockSpec((ts, D_B), lambda i: (i, OFF_BV // D_B)),
                  pl.BlockSpec((1, D_B), lambda i: (0, 0)),
                  pl.BlockSpec((1, D_B), lambda i: (0, 0)),
                  pl.BlockSpec((SG_GROUPS, SG_CHUNK, SG_CHUNK), lambda i: (0, 0, 0)),
                  pl.BlockSpec((SG_CHUNK, SG_GROUPS), lambda i: (0, 0))],
        out_specs=pl.BlockSpec((ts, D_B), lambda i: (i, 0)),
        out_shape=jax.ShapeDtypeStruct((t, D_B), BF16),
        compiler_params=_cparams(("parallel",)),
        name="spatial_gating",
    )(hp, hp, ln_g.reshape(1, D_B), ln_b.reshape(1, D_B), w_s, b_s.T)


def _dsa_kernel(iq_ref, q_ref, gq_ref, k_ref, v_ref, gk_ref, cq_ref, sq_ref, ciq_ref, siq_ref,
                ck_ref, sk_ref, cik_ref, sik_ref, o_ref,
                kr_s, vt_s, iklo_s, ikhi_s, sc_s, acc_s, *, kc, n_sel, n_bisect):
    qi = pl.program_id(1)
    tq = q_ref.shape[0]
    s_len = k_ref.shape[0]
    lane = lax.broadcasted_iota(jnp.int32, (1, HEAD_DIM), 1)

    def rope128(x, cos, sin):
        return x * cos + pltpu.roll(x, HEAD_DIM // 2, axis=1) * sin

    def rope64(x, cos, sin):
        half = IDX_DIM // 2
        partner = jnp.where(lane % IDX_DIM < half, pltpu.roll(x, HEAD_DIM - half, axis=1),
                            pltpu.roll(x, half, axis=1))
        return x * cos + partner * sin

    @pl.when(qi == 0)
    def _():
        kr_s[...] = rope128(k_ref[...], ck_ref[...], sk_ref[...]).astype(BF16)
        for c in range(s_len // kc):
            vt_s[c] = v_ref[c * kc:(c + 1) * kc, :].T.astype(BF16)
        ik = jnp.where(lane < IDX_DIM, rope64(gk_ref[...], cik_ref[...], sik_ref[...]), 0.0)
        iklo_s[...] = ik.astype(BF16)
        ikhi_s[...] = pltpu.roll(ik, IDX_DIM, axis=1).astype(BF16)

    hsel = lax.broadcasted_iota(jnp.int32, (IDX_HEADS, HEAD_DIM), 0)
    lsel = lax.broadcasted_iota(jnp.int32, (IDX_HEADS, HEAD_DIM), 1)
    pick = (lsel == hsel + GATE_IW).astype(F32)
    iw_t = _dot_nt(pick, gq_ref[...], precision=lax.Precision.HIGHEST) * ((IDX_HEADS ** -0.5) * (IDX_DIM ** -0.5))

    n_kc = ((qi + 1) * tq + kc - 1) // kc
    q_pos = qi * tq + lax.broadcasted_iota(jnp.int32, (1, tq), 1)

    iq_pairs = []
    for p in range(IDX_HEADS // 2):
        ls = slice(p * HEAD_DIM, (p + 1) * HEAD_DIM)
        iq_pairs.append(rope64(iq_ref[:, ls], ciq_ref[...], siq_ref[...]).astype(BF16))

    def score_chunk(j, carry):
        smin, smax = carry
        r0 = pl.multiple_of(j * kc, kc)
        rows = pl.ds(r0, kc)
        ik_lo = iklo_s[rows, :]
        ik_hi = ikhi_s[rows, :]
        acc = jnp.zeros((kc, tq), F32)
        for hh in range(IDX_HEADS):
            ikx = ik_lo if hh % 2 == 0 else ik_hi
            rel = jnp.maximum(_dot_nt(ikx, iq_pairs[hh // 2]), 0.0)
            acc = acc + rel * iw_t[hh:hh + 1, :]
        k_pos = r0 + lax.broadcasted_iota(jnp.int32, (kc, 1), 0)
        adm = k_pos <= q_pos
        sc_s[rows, :] = jnp.where(adm, acc, NEG_INF)
        smin = jnp.minimum(smin, jnp.min(jnp.where(adm, acc, jnp.inf), axis=0, keepdims=True))
        smax = jnp.maximum(smax, jnp.max(jnp.where(adm, acc, NEG_INF), axis=0, keepdims=True))
        return smin, smax

    lo, hi = lax.fori_loop(0, n_kc, score_chunk,
                           (jnp.full((1, tq), jnp.inf, F32), jnp.full((1, tq), NEG_INF, F32)))

    def count_ge(thr):
        def body(j, c):
            rows = pl.ds(pl.multiple_of(j * kc, kc), kc)
            hit = jnp.where(sc_s[rows, :] >= thr, 1.0, 0.0)
            return c + jnp.sum(hit.reshape(kc // 8, 8, tq), axis=0)
        return jnp.sum(lax.fori_loop(0, n_kc, body, jnp.zeros((8, tq), F32)), axis=0, keepdims=True)

    def halve(state):
        lo_, hi_, c_lo = state
        mid = 0.5 * (lo_ + hi_)
        c_mid = count_ge(mid)
        ok = c_mid >= n_sel
        return jnp.where(ok, mid, lo_), jnp.where(ok, hi_, mid), jnp.where(ok, c_mid, c_lo)

    def search_more(carry):
        it, (_, _, c_lo) = carry
        return jnp.logical_and(it < n_bisect, jnp.max(c_lo) > n_sel)

    def search(carry):
        it, state = carry
        return it + 2, halve(halve(state))

    n_adm = (q_pos + 1).astype(F32)
    _, (lo, hi, _) = lax.while_loop(search_more, search, (jnp.int32(0), (lo, hi, n_adm)))

    qscale = (HEAD_DIM ** -0.5) * float(np.log2(np.e))
    q_all = jnp.concatenate(
        [(rope128(q_ref[:, h * HEAD_DIM:(h + 1) * HEAD_DIM], cq_ref[...], sq_ref[...]) * qscale).astype(BF16)
         for h in range(SA_HEADS)], axis=0)
    acc_s[...] = jnp.zeros_like(acc_s)

    def att_chunk(j, carry):
        m_, l_ = carry
        rows = pl.ds(pl.multiple_of(j * kc, kc), kc)
        bias = jnp.where(sc_s[rows, :] >= lo, 0.0, NEG_INF)
        logit = _dot_nt(kr_s[rows, :], q_all) + jnp.concatenate([bias] * SA_HEADS, axis=1)
        m_new = jnp.maximum(m_, jnp.max(logit, axis=0, keepdims=True))
        m_use = jnp.where(m_new == NEG_INF, 0.0, m_new)
        alpha = jnp.exp2(m_ - m_use)
        p = jnp.exp2(logit - m_use)
        acc_s[...] = alpha * acc_s[...] + _dot(vt_s[j], p.astype(BF16))
        return m_new, alpha * l_ + jnp.sum(p, axis=0, keepdims=True)

    _, l = lax.fori_loop(0, n_kc, att_chunk,
                         (jnp.full((1, SA_HEADS * tq), NEG_INF, F32), jnp.zeros((1, SA_HEADS * tq), F32)))
    out_t = acc_s[...] / l
    for h in range(SA_HEADS):
        o_ref[:, h * HEAD_DIM:(h + 1) * HEAD_DIM] = out_t[:, h * tq:(h + 1) * tq].T.astype(o_ref.dtype)


def _rope_tables(s_, dim, reps):
    inv_freq = ROPE_THETA ** (-jnp.arange(0, dim, 2, dtype=F32) / dim)
    ang = jnp.arange(s_, dtype=F32)[:, None] * inv_freq[None, :]
    cos, sin = jnp.cos(ang), jnp.sin(ang)
    return (jnp.tile(jnp.concatenate([cos, cos], axis=-1), (1, reps)),
            jnp.tile(jnp.concatenate([-sin, sin], axis=-1), (1, reps)))


def _dsa(hp, b_, s_, tq, kc):
    n_sel = min(INDEX_TOPK, s_ // 4)
    cos_h, sin_h = _rope_tables(s_, HEAD_DIM, 1)
    cos_i, sin_i = _rope_tables(s_, IDX_DIM, 2)
    nq = s_ // tq
    qrow = lambda b, i: b * nq + i
    tab_q = pl.BlockSpec((tq, HEAD_DIM), lambda b, i: (i, 0))
    tab_k = pl.BlockSpec((s_, HEAD_DIM), lambda b, i: (0, 0))
    return pl.pallas_call(
        functools.partial(_dsa_kernel, kc=kc, n_sel=n_sel, n_bisect=40),
        grid=(b_, nq),
        in_specs=[pl.BlockSpec((tq, IDX_HEADS * IDX_DIM), lambda b, i: (qrow(b, i), OFF_IQ // (IDX_HEADS * IDX_DIM))),
                  pl.BlockSpec((tq, D_C), lambda b, i: (qrow(b, i), OFF_CQ // D_C)),
                  pl.BlockSpec((tq, HEAD_DIM), lambda b, i: (qrow(b, i), OFF_GATE // HEAD_DIM)),
                  pl.BlockSpec((s_, HEAD_DIM), lambda b, i: (b, OFF_CK // HEAD_DIM)),
                  pl.BlockSpec((s_, HEAD_DIM), lambda b, i: (b, OFF_CV // HEAD_DIM)),
                  pl.BlockSpec((s_, HEAD_DIM), lambda b, i: (b, OFF_GATE // HEAD_DIM)),
                  tab_q, tab_q, tab_q, tab_q, tab_k, tab_k, tab_k, tab_k],
        out_specs=pl.BlockSpec((tq, D_C), lambda b, i: (qrow(b, i), 0)),
        out_shape=jax.ShapeDtypeStruct((b_ * s_, D_C), BF16),
        scratch_shapes=[pltpu.VMEM((s_, HEAD_DIM), BF16),
                        pltpu.VMEM((s_ // kc, HEAD_DIM, kc), BF16),
                        pltpu.VMEM((s_, HEAD_DIM), BF16),
                        pltpu.VMEM((s_, HEAD_DIM), BF16),
                        pltpu.VMEM((s_, tq), F32),
                        pltpu.VMEM((HEAD_DIM, SA_HEADS * tq), F32)],
        compiler_params=_cparams(("parallel", "arbitrary")),
        name="dsa",
    )(hp, hp, hp, hp, hp, hp, cos_h, sin_h, cos_i, sin_i, cos_h, sin_h, cos_i, sin_i)


def _out_proj_kernel(x_ref, a_ref, b_ref, c_ref, w_ref, o_ref):
    o_ref[...] = (x_ref[...] + _dot(a_ref[...], w_ref[:D_A, :]) + _dot(b_ref[...], w_ref[D_A:D_A + D_B, :])
                  + _dot(c_ref[...], w_ref[D_A + D_B:, :]))


def _out_proj(x, oa, ob, oc, w_out, l, tm):
    t, d = x.shape
    row = lambda w: pl.BlockSpec((tm, w), lambda i: (i, 0))
    return pl.pallas_call(
        _out_proj_kernel,
        grid=(t // tm,),
        in_specs=[row(d), row(D_A), row(D_B), row(D_C),
                  pl.BlockSpec((None, D_A + D_B + D_C, d), lambda i: (l, 0, 0))],
        out_specs=row(d),
        out_shape=jax.ShapeDtypeStruct((t, d), F32),
        compiler_params=_cparams(("parallel",)),
        name="out_proj",
    )(x, oa, ob, oc, w_out)


def _cross_kernel(x_ref, g_ref, kv_ref, wq_ref, wo_ref, o_ref):
    x = x_ref[...]
    xn = _rms_rows(x, g_ref[...]).astype(BF16)
    q = _dot(xn, wq_ref[...])
    outs = []
    for h in range(MEM_HEADS):
        ls = slice(h * HEAD_DIM, (h + 1) * HEAD_DIM)
        kh = kv_ref[:, ls]
        vh = kv_ref[:, D_MEM + h * HEAD_DIM:D_MEM + (h + 1) * HEAD_DIM]
        logit = _dot_nt(q[:, ls].astype(BF16), kh) * (HEAD_DIM ** -0.5)
        p = jnp.exp(logit - jnp.max(logit, axis=-1, keepdims=True))
        l = jnp.sum(p, axis=-1, keepdims=True)
        outs.append(_dot(p.astype(BF16), vh) / l)
    o = jnp.concatenate(outs, axis=-1).astype(BF16)
    o_ref[...] = x + _dot(o, wo_ref[...])


def _cross(x, g, kv, w_q, w_o, l, b_, s_, tm):
    d = x.shape[1]
    m_len = kv.shape[0] // b_
    nt = s_ // tm
    return pl.pallas_call(
        _cross_kernel,
        grid=(b_, nt),
        in_specs=[pl.BlockSpec((tm, d), lambda b, i: (b * nt + i, 0)),
                  pl.BlockSpec((1, d), lambda b, i: (0, 0)),
                  pl.BlockSpec((m_len, 2 * D_MEM), lambda b, i: (b, 0)),
                  pl.BlockSpec((None, d, D_MEM), lambda b, i: (l, 0, 0)),
                  pl.BlockSpec((None, D_MEM, d), lambda b, i: (l, 0, 0))],
        out_specs=pl.BlockSpec((tm, d), lambda b, i: (b * nt + i, 0)),
        out_shape=jax.ShapeDtypeStruct(x.shape, F32),
        compiler_params=_cparams(("parallel", "parallel")),
        name="cross_attn",
    )(x, g.reshape(1, d), kv, w_q, w_o)


def _mlp_kernel(x_ref, g_ref, wu_ref, wd_ref, o_ref, xn_ref, acc_ref):
    j = pl.program_id(1)

    @pl.when(j == 0)
    def _():
        xn_ref[...] = _rms_rows(x_ref[...], g_ref[...]).astype(BF16)
        acc_ref[...] = jnp.zeros_like(acc_ref)

    hid = jnp.maximum(_dot(xn_ref[...], wu_ref[...]), 0.0)
    acc_ref[...] += _dot((hid * hid).astype(BF16), wd_ref[...])

    @pl.when(j == pl.num_programs(1) - 1)
    def _():
        o_ref[...] = x_ref[...] + acc_ref[...]


def _mlp(x, g, w_up, w_down, l, tm, tf):
    t, d = x.shape
    f = w_up.shape[2]
    return pl.pallas_call(
        _mlp_kernel,
        grid=(t // tm, f // tf),
        in_specs=[pl.BlockSpec((tm, d), lambda i, j: (i, 0)),
                  pl.BlockSpec((1, d), lambda i, j: (0, 0)),
                  pl.BlockSpec((None, d, tf), lambda i, j: (l, 0, j)),
                  pl.BlockSpec((None, tf, d), lambda i, j: (l, j, 0))],
        out_specs=pl.BlockSpec((tm, d), lambda i, j: (i, 0)),
        out_shape=jax.ShapeDtypeStruct((t, d), F32),
        scratch_shapes=[pltpu.VMEM((tm, d), BF16), pltpu.VMEM((tm, d), F32)],
        compiler_params=_cparams(("parallel", "arbitrary")),
        name="mlp",
    )(x, g.reshape(1, d), w_up, w_down)


def _final_norm_kernel(x_ref, g_ref, o_ref):
    o_ref[...] = _rms_rows(x_ref[...], g_ref[...])


def _final_norm(x, g, tm):
    t, d = x.shape
    return pl.pallas_call(
        _final_norm_kernel,
        grid=(t // tm,),
        in_specs=[pl.BlockSpec((tm, d), lambda i: (i, 0)), pl.BlockSpec((1, d), lambda i: (0, 0))],
        out_specs=pl.BlockSpec((tm, d), lambda i: (i, 0)),
        out_shape=jax.ShapeDtypeStruct((t, d), F32),
        compiler_params=_cparams(("parallel",)),
        name="final_norm",
    )(x, g.reshape(1, d))


def _w_in_moves():
    widths = (3 * D_A, D_A, GDN_HEADS, GDN_HEADS, D_B, D_B, D_C, HEAD_DIM, HEAD_DIM,
              IDX_HEADS * IDX_DIM, IDX_DIM, IDX_HEADS)
    a_qkv, a_z, a_dec, _, b_u, b_v, c_q, c_k, c_v, i_q, i_k, _ = (int(v) for v in np.cumsum((0,) + widths)[:-1])
    return ((i_q, IDX_HEADS * IDX_DIM, OFF_IQ), (b_u, D_B, OFF_BU), (b_v, D_B, OFF_BV), (c_k, HEAD_DIM, OFF_CK),
            (c_v, HEAD_DIM, OFF_CV), (c_q, D_C, OFF_CQ), (a_qkv, 3 * D_A, OFF_AQKV), (a_z, D_A, OFF_AZ),
            (i_k, IDX_DIM + IDX_HEADS, OFF_GATE + GATE_IK), (a_dec, 2 * GDN_HEADS, OFF_GATE + GATE_DEC))


def _prep_w_in_kernel(w_ref, o_ref):
    o_ref[:, OFF_GATE:] = jnp.zeros((o_ref.shape[0], D_IN_PAD - OFF_GATE), o_ref.dtype)
    for src, width, dst in _w_in_moves():
        o_ref[:, dst:dst + width] = w_ref[:, src:src + width].astype(o_ref.dtype)


def _prep_w_in(w_in, dtype=BF16):
    depth, d, n = w_in.shape
    rt = _tile(d, 256)
    return pl.pallas_call(
        _prep_w_in_kernel,
        grid=(depth, d // rt),
        in_specs=[pl.BlockSpec((None, rt, n), lambda l, i: (l, i, 0))],
        out_specs=pl.BlockSpec((None, rt, D_IN_PAD), lambda l, i: (l, i, 0)),
        out_shape=jax.ShapeDtypeStruct((depth, d, D_IN_PAD), dtype),
        compiler_params=_cparams(("parallel", "parallel")),
        name="prep_w_in",
    )(w_in)


def _tile(n, pref):
    t = min(n, pref)
    while n % t:
        t //= 2
    return t


def kernel(x, mem, ln_mix, w_in, conv_w, a_log, dt_bias, gdn_norm, sg_ln_g, sg_ln_b, sg_w, sg_b, w_out, ln_cross,
           ln_mem, w_q_mem, w_kv_mem, w_o_mem, ln_mlp, w_up, w_down, ln_final):
    b_, s_, d = x.shape
    m_len = mem.shape[1]
    depth = w_in.shape[0]
    t = b_ * s_
    tm = _tile(t, 512)
    xf = x.reshape(t, d)
    memf = mem.reshape(b_ * m_len, d)
    w_in_p = _prep_w_in(w_in)
    w_out_b, w_kv_b, w_q_b, w_o_b = (w.astype(BF16) for w in (w_out, w_kv_mem, w_q_mem, w_o_mem))
    w_up_b, w_down_b = w_up.astype(BF16), w_down.astype(BF16)
    for l in range(depth):
        hp = _norm_proj(xf, ln_mix[l], w_in_p, l, _tile(t, 1024), D_IN_PAD // 5, F32)
        oa = _gdn(hp, conv_w[l], a_log[l], dt_bias[l], gdn_norm[l], b_, s_)
        ob = _sg(hp, sg_ln_g[l], sg_ln_b[l], sg_w[l], sg_b[l], _tile(s_, 512))
        oc = _dsa(hp, b_, s_, _tile(s_, 256), _tile(s_, 256))
        xf = _out_proj(xf, oa, ob, oc, w_out_b, l, tm)
        kv = _norm_proj(memf, ln_mem[l], w_kv_b, l, _tile(b_ * m_len, 512), 2 * D_MEM, BF16)
        xf = _cross(xf, ln_cross[l], kv, w_q_b, w_o_b, l, b_, s_, _tile(s_, 512))
        xf = _mlp(xf, ln_mlp[l], w_up_b, w_down_b, l, tm, 1024)
    return _final_norm(xf, ln_final, tm).reshape(b_, s_, d)
```

```python
import functools

import jax
import jax.numpy as jnp
import numpy as np
from jax import lax
from jax.experimental import pallas as pl
from jax.experimental.pallas import tpu as pltpu

F32 = jnp.float32
BF16 = jnp.bfloat16

D_MODEL = 2048
HEAD_DIM = 128
GDN_HEADS = 6
GDN_CHUNK = 128
INV_BASE = 8
GDN_GROUP_ROWS = 512
CONV_K = 4
CONV_PAD = 8
SG_GROUPS = 4
SG_CHUNK = 128
SA_HEADS = 6
IDX_HEADS = 16
IDX_DIM = 64
INDEX_TOPK = 256
MEM_HEADS = 4
D_FF = 4 * D_MODEL
ROPE_THETA = 10000.0
NORM_EPS = 1e-6

D_A = GDN_HEADS * HEAD_DIM
D_B = SG_GROUPS * HEAD_DIM
D_C = SA_HEADS * HEAD_DIM
D_MEM = MEM_HEADS * HEAD_DIM

OFF_IQ = 0
OFF_BU = 1024
OFF_BV = 1536
OFF_CK = 2048
OFF_CV = 2176
OFF_CQ = 2304
OFF_AQKV = 3072
OFF_AZ = 5376
OFF_GATE = 6144
D_IN_PAD = 6400
GATE_IK = 0
GATE_IW = IDX_DIM
GATE_DEC = IDX_DIM + IDX_HEADS
GATE_BETA = GATE_DEC + GDN_HEADS

VMEM_LIMIT = 56 * 1024 * 1024
NEG_INF = float("-inf")


def _cparams(sem):
    return pltpu.CompilerParams(dimension_semantics=sem, vmem_limit_bytes=VMEM_LIMIT)


def _rms_rows(x, g):
    ms = jnp.mean(x * x, axis=-1, keepdims=True)
    return x * lax.rsqrt(ms + NORM_EPS) * g


def _dot(a, b):
    return jnp.dot(a, b, preferred_element_type=F32)


def _dot_nt(a, b, precision=None):
    return lax.dot_general(a, b, (((1,), (1,)), ((), ())), preferred_element_type=F32, precision=precision)


def _dot_tn(a, b):
    return lax.dot_general(a, b, (((0,), (0,)), ((), ())), preferred_element_type=F32)


def _bmm(a, b, precision=None):
    return lax.dot_general(a, b, (((2,), (1,)), ((0,), (0,))), preferred_element_type=F32, precision=precision)


def _bmm_nt(a, b, precision=None):
    return lax.dot_general(a, b, (((2,), (2,)), ((0,), (0,))), preferred_element_type=F32, precision=precision)


def _sigmoid(x):
    return 1.0 / (1.0 + jnp.exp(-x))


def _norm_proj_kernel(x_ref, g_ref, w_ref, o_ref, xn_ref):
    @pl.when(pl.program_id(1) == 0)
    def _():
        xn_ref[...] = _rms_rows(x_ref[...], g_ref[...]).astype(BF16)

    o_ref[...] = _dot(xn_ref[...], w_ref[...]).astype(o_ref.dtype)


def _norm_proj(x, g, w, l, tm, tn, out_dtype):
    t, d = x.shape
    n = w.shape[2]
    return pl.pallas_call(
        _norm_proj_kernel,
        grid=(t // tm, n // tn),
        in_specs=[pl.BlockSpec((tm, d), lambda i, j: (i, 0)),
                  pl.BlockSpec((1, d), lambda i, j: (0, 0)),
                  pl.BlockSpec((None, d, tn), lambda i, j: (l, 0, j))],
        out_specs=pl.BlockSpec((tm, tn), lambda i, j: (i, j)),
        out_shape=jax.ShapeDtypeStruct((t, n), out_dtype),
        scratch_shapes=[pltpu.VMEM((tm, d), BF16)],
        compiler_params=_cparams(("parallel", "arbitrary")),
        name="norm_proj",
    )(x, g.reshape(1, d), w)


def _gdn_kernel(alog_ref, dtb_ref, q_ref, k_ref, v_ref, z_ref, gate_ref, cwq_ref, cwk_ref, cwv_ref, gn_ref,
                o_ref, u_s, w_s, qd_s, kd_s, at_s, gl_s, o_s, pad_s, *, group, heads):
    s_len = q_ref.shape[0]
    c_len = GDN_CHUNK
    n_chunks = s_len // c_len
    row = lax.broadcasted_iota(jnp.int32, (s_len, 1), 0)
    pos = row % c_len
    lane = lax.broadcasted_iota(jnp.int32, (1, HEAD_DIM), 1)
    lane3 = lax.broadcasted_iota(jnp.int32, (1, 1, HEAD_DIM), 2)
    ci = lax.broadcasted_iota(jnp.int32, (c_len, c_len), 0)
    cj = lax.broadcasted_iota(jnp.int32, (c_len, c_len), 1)
    tril = (cj <= ci)[None]
    strict = (cj < ci)[None]
    base_blk = (ci // INV_BASE == cj // INV_BASE)[None]
    merge_blks = []
    size = INV_BASE
    while size < c_len:
        merge_blks.append(((ci // (2 * size) == cj // (2 * size)) & (ci // size > cj // size))[None])
        size *= 2
    rows_g = group * c_len

    pad_s[0:CONV_PAD, :] = jnp.zeros((CONV_PAD, HEAD_DIM), F32)

    def conv_silu(x, cw):
        pad_s[CONV_PAD:, :] = x
        y = x * cw[CONV_K - 1:CONV_K, :]
        for j in range(1, CONV_K):
            y = y + pad_s[CONV_PAD - j:CONV_PAD - j + s_len, :] * cw[CONV_K - 1 - j:CONV_K - j, :]
        return y * _sigmoid(y)

    def l2n(x):
        return x * lax.rsqrt(jnp.sum(x * x, axis=-1, keepdims=True) + NORM_EPS)

    gate = gate_ref[...]
    for hl in range(heads):
        h = pl.program_id(1) * heads + hl
        cs = slice(hl * HEAD_DIM, (hl + 1) * HEAD_DIM)
        q = l2n(conv_silu(q_ref[:, cs], cwq_ref[:, cs])) * (HEAD_DIM ** -0.5)
        k = l2n(conv_silu(k_ref[:, cs], cwk_ref[:, cs]))
        v = conv_silu(v_ref[:, cs], cwv_ref[:, cs])

        dec = jnp.sum(jnp.where(lane == GATE_DEC + h, gate, 0.0), axis=-1, keepdims=True)
        bet = jnp.sum(jnp.where(lane == GATE_BETA + h, gate, 0.0), axis=-1, keepdims=True)
        beta = jnp.broadcast_to(_sigmoid(bet), (s_len, HEAD_DIM))
        a_neg = -jnp.exp(jnp.full((1, 1), alog_ref[h], F32))
        xg = dec + dtb_ref[h]
        softplus = jnp.maximum(xg, 0.0) + jnp.log(1.0 + jnp.exp(-jnp.abs(xg)))
        g = jnp.broadcast_to(a_neg * softplus, (s_len, HEAD_DIM))

        gc = g
        shift = 1
        while shift < c_len:
            gc = gc + jnp.where(pos >= shift, pltpu.roll(gc, shift, axis=0), 0.0)
            shift *= 2

        kb = k * beta
        eg = jnp.exp(gc)
        vb = v * beta
        kbe = kb * eg
        qd_s[hl] = (q * eg).astype(BF16)

        for gi in range(n_chunks // group):
            sl = slice(gi * rows_g, (gi + 1) * rows_g)

            def r3(t):
                return t[sl].reshape(group, c_len, t.shape[-1])

            gc3 = r3(gc)
            k3 = r3(k)
            pmat = jnp.where(lane3 == 0, gc3, jnp.where(lane3 == 1, 1.0, 0.0))
            qmat = jnp.where(lane3 == 0, 1.0, jnp.where(lane3 == 1, -gc3, 0.0))
            dlog = _bmm_nt(pmat, qmat, precision=lax.Precision.HIGHEST)
            decay = jnp.where(tril, jnp.exp(jnp.where(tril, dlog, 0.0)), 0.0)
            k3b = k3.astype(BF16)
            amat = jnp.where(strict, _bmm_nt(r3(kb).astype(BF16), k3b) * decay, 0.0)
            attn = _bmm_nt(r3(q).astype(BF16), k3b) * decay
            bpow = jnp.where(base_blk, -amat, 0.0)
            tlow = bpow
            span = 2
            while span < INV_BASE:
                bb = bpow.astype(BF16)
                bpow = _bmm(bb, bb)
                tlow = tlow + bpow + _bmm(bpow.astype(BF16), tlow.astype(BF16))
                span *= 2
            for off_blk in merge_blks:
                a_off = jnp.where(off_blk, amat, 0.0).astype(BF16)
                tb = tlow.astype(BF16)
                xmat = a_off + _bmm(tb, a_off)
                tlow = tlow - xmat - _bmm(xmat.astype(BF16), tb)
            rhs = jnp.concatenate([r3(vb), r3(kbe)], axis=-1)
            sol = rhs + _bmm(tlow.astype(BF16), rhs.astype(BF16))
            glast = gc3[:, c_len - 1:c_len, :]
            u_s[hl, sl, :] = sol[:, :, :HEAD_DIM].reshape(rows_g, HEAD_DIM)
            w_s[hl, sl, :] = sol[:, :, HEAD_DIM:].reshape(rows_g, HEAD_DIM).astype(BF16)
            kd_s[hl, sl, :] = (k3 * jnp.exp(glast - gc3)).reshape(rows_g, HEAD_DIM).astype(BF16)
            at_s[hl, sl, :] = attn.reshape(rows_g, c_len).astype(BF16)
            gl_s[hl, gi * group * 8:(gi + 1) * group * 8, :] = jnp.broadcast_to(
                jnp.exp(glast), (group, 8, HEAD_DIM)).reshape(group * 8, HEAD_DIM)

    def step(n, states):
        rows = pl.ds(pl.multiple_of(n * c_len, c_len), c_len)
        new_states = []
        for hl in range(heads):
            sb = states[hl].astype(BF16)
            v_new = u_s[hl, rows, :] - _dot(w_s[hl, rows, :], sb)
            vnb = v_new.astype(BF16)
            o_s[hl, rows, :] = _dot(qd_s[hl, rows, :], sb) + _dot(at_s[hl, rows, :], vnb)
            gl = jnp.broadcast_to(gl_s[hl, pl.ds(pl.multiple_of(n * 8, 8), 1), :], (HEAD_DIM, HEAD_DIM))
            new_states.append(states[hl] * gl + _dot_tn(kd_s[hl, rows, :], vnb))
        return tuple(new_states)

    lax.fori_loop(0, n_chunks, step, tuple(jnp.zeros((HEAD_DIM, HEAD_DIM), F32) for _ in range(heads)))

    for hl in range(heads):
        cs = slice(hl * HEAD_DIM, (hl + 1) * HEAD_DIM)
        z = z_ref[:, cs]
        o_ref[:, cs] = (_rms_rows(o_s[hl], gn_ref[...]) * (z * _sigmoid(z))).astype(o_ref.dtype)


def _gdn(hp, conv_w, a_log, dt_bias, gdn_norm, b_, s_, heads=2):
    n_chunks = s_ // GDN_CHUNK
    group = min(GDN_GROUP_ROWS // GDN_CHUNK, n_chunks)
    hw = heads * HEAD_DIM
    hb = lambda off: (lambda b, h: (b, off // hw + h))
    cb = lambda off: (lambda b, h: (0, off // hw + h))
    smem = pl.BlockSpec(memory_space=pltpu.SMEM)
    return pl.pallas_call(
        functools.partial(_gdn_kernel, group=group, heads=heads),
        grid=(b_, GDN_HEADS // heads),
        in_specs=[smem, smem,
                  pl.BlockSpec((s_, hw), hb(OFF_AQKV)),
                  pl.BlockSpec((s_, hw), hb(OFF_AQKV + D_A)),
                  pl.BlockSpec((s_, hw), hb(OFF_AQKV + 2 * D_A)),
                  pl.BlockSpec((s_, hw), hb(OFF_AZ)),
                  pl.BlockSpec((s_, HEAD_DIM), lambda b, h: (b, OFF_GATE // HEAD_DIM)),
                  pl.BlockSpec((CONV_K, hw), cb(0)),
                  pl.BlockSpec((CONV_K, hw), cb(D_A)),
                  pl.BlockSpec((CONV_K, hw), cb(2 * D_A)),
                  pl.BlockSpec((1, HEAD_DIM), lambda b, h: (0, 0))],
        out_specs=pl.BlockSpec((s_, hw), lambda b, h: (b, h)),
        out_shape=jax.ShapeDtypeStruct((b_ * s_, D_A), BF16),
        scratch_shapes=[pltpu.VMEM((heads, s_, HEAD_DIM), F32),
                        pltpu.VMEM((heads, s_, HEAD_DIM), BF16),
                        pltpu.VMEM((heads, s_, HEAD_DIM), BF16),
                        pltpu.VMEM((heads, s_, HEAD_DIM), BF16),
                        pltpu.VMEM((heads, s_, GDN_CHUNK), BF16),
                        pltpu.VMEM((heads, n_chunks * 8, HEAD_DIM), F32),
                        pltpu.VMEM((heads, s_, HEAD_DIM), F32),
                        pltpu.VMEM((CONV_PAD + s_, HEAD_DIM), F32)],
        compiler_params=_cparams(("parallel", "arbitrary")),
        name="gdn",
    )(a_log, dt_bias, hp, hp, hp, hp, hp, conv_w, conv_w, conv_w, gdn_norm.reshape(1, HEAD_DIM))


def _gelu_tanh(x):
    c = float(np.sqrt(2.0 / np.pi))
    return 0.5 * x * (1.0 + jnp.tanh(c * (x + 0.044715 * (x * x * x))))


def _sg_kernel(u_ref, v_ref, lg_ref, lb_ref, w_ref, bt_ref, o_ref):
    ts = u_ref.shape[0]
    v = _gelu_tanh(v_ref[...])
    mu = jnp.mean(v, axis=-1, keepdims=True)
    vc = v - mu
    var = jnp.mean(vc * vc, axis=-1, keepdims=True)
    vn = (vc * lax.rsqrt(var + NORM_EPS) * lg_ref[...] + lb_ref[...]).astype(BF16)
    ti = lax.broadcasted_iota(jnp.int32, (SG_CHUNK, SG_CHUNK), 0)
    si = lax.broadcasted_iota(jnp.int32, (SG_CHUNK, SG_CHUNK), 1)
    bt = bt_ref[...]
    for g in range(SG_GROUPS):
        wg = jnp.where(si <= ti, w_ref[g], 0.0).astype(BF16)
        bias = jnp.broadcast_to(bt[:, g:g + 1], (SG_CHUNK, HEAD_DIM))
        cs = slice(g * HEAD_DIM, (g + 1) * HEAD_DIM)
        for c in range(ts // SG_CHUNK):
            rs = slice(c * SG_CHUNK, (c + 1) * SG_CHUNK)
            s = _dot(wg, vn[rs, cs]) + bias
            o_ref[rs, cs] = (_gelu_tanh(u_ref[rs, cs]) * s).astype(o_ref.dtype)


def _sg(hp, ln_g, ln_b, w_s, b_s, ts):
    t = hp.shape[0]
    return pl.pallas_call(
        _sg_kernel,
        grid=(t // ts,),
        in_specs=[pl.BlockSpec((ts, D_B), lambda i: (i, OFF_BU // D_B)),
                  pl.BlockSpec((ts, D_B), lambda i: (i, OFF_BV // D_B)),
                  pl.BlockSpec((1, D_B), lambda i: (0, 0)),
                  pl.BlockSpec((1, D_B), lambda i: (0, 0)),
                  pl.BlockSpec((SG_GROUPS, SG_CHUNK, SG_CHUNK), lambda i: (0, 0, 0)),
                  pl.BlockSpec((SG_CHUNK, SG_GROUPS), lambda i: (0, 0))],
        out_specs=pl.BlockSpec((ts, D_B), lambda i: (i, 0)),
        out_shape=jax.ShapeDtypeStruct((t, D_B), BF16),
        compiler_params=_cparams(("parallel",)),
        name="spatial_gating",
    )(hp, hp, ln_g.reshape(1, D_B), ln_b.reshape(1, D_B), w_s, b_s.T)


def _dsa_kernel(iq_ref, q_ref, gq_ref, k_ref, v_ref, gk_ref, cq_ref, sq_ref, ciq_ref, siq_ref,
                ck_ref, sk_ref, cik_ref, sik_ref, o_ref,
                kr_s, vt_s, iklo_s, ikhi_s, sc_s, acc_s, *, kc, n_sel, n_bisect):
    qi = pl.program_id(1)
    tq = q_ref.shape[0]
    s_len = k_ref.shape[0]
    lane = lax.broadcasted_iota(jnp.int32, (1, HEAD_DIM), 1)

    def rope128(x, cos, sin):
        return x * cos + pltpu.roll(x, HEAD_DIM // 2, axis=1) * sin

    def rope64(x, cos, sin):
        half = IDX_DIM // 2
        partner = jnp.where(lane % IDX_DIM < half, pltpu.roll(x, HEAD_DIM - half, axis=1),
                            pltpu.roll(x, half, axis=1))
        return x * cos + partner * sin

    @pl.when(qi == 0)
    def _():
        kr_s[...] = rope128(k_ref[...], ck_ref[...], sk_ref[...]).astype(BF16)
        for c in range(s_len // kc):
            vt_s[c] = v_ref[c * kc:(c + 1) * kc, :].T.astype(BF16)
        ik = jnp.where(lane < IDX_DIM, rope64(gk_ref[...], cik_ref[...], sik_ref[...]), 0.0)
        iklo_s[...] = ik.astype(BF16)
        ikhi_s[...] = pltpu.roll(ik, IDX_DIM, axis=1).astype(BF16)

    hsel = lax.broadcasted_iota(jnp.int32, (IDX_HEADS, HEAD_DIM), 0)
    lsel = lax.broadcasted_iota(jnp.int32, (IDX_HEADS, HEAD_DIM), 1)
    pick = (lsel == hsel + GATE_IW).astype(F32)
    iw_t = _dot_nt(pick, gq_ref[...], precision=lax.Precision.HIGHEST) * ((IDX_HEADS ** -0.5) * (IDX_DIM ** -0.5))

    n_kc = ((qi + 1) * tq + kc - 1) // kc
    q_pos = qi * tq + lax.broadcasted_iota(jnp.int32, (1, tq), 1)

    iq_pairs = []
    for p in range(IDX_HEADS // 2):
        ls = slice(p * HEAD_DIM, (p + 1) * HEAD_DIM)
        iq_pairs.append(rope64(iq_ref[:, ls], ciq_ref[...], siq_ref[...]).astype(BF16))

    def score_chunk(j, carry):
        smin, smax = carry
        r0 = pl.multiple_of(j * kc, kc)
        rows = pl.ds(r0, kc)
        ik_lo = iklo_s[rows, :]
        ik_hi = ikhi_s[rows, :]
        acc = jnp.zeros((kc, tq), F32)
        for hh in range(IDX_HEADS):
            ikx = ik_lo if hh % 2 == 0 else ik_hi
            rel = jnp.maximum(_dot_nt(ikx, iq_pairs[hh // 2]), 0.0)
            acc = acc + rel * iw_t[hh:hh + 1, :]
        k_pos = r0 + lax.broadcasted_iota(jnp.int32, (kc, 1), 0)
        adm = k_pos <= q_pos
        sc_s[rows, :] = jnp.where(adm, acc, NEG_INF)
        smin = jnp.minimum(smin, jnp.min(jnp.where(adm, acc, jnp.inf), axis=0, keepdims=True))
        smax = jnp.maximum(smax, jnp.max(jnp.where(adm, acc, NEG_INF), axis=0, keepdims=True))
        return smin, smax

    lo, hi = lax.fori_loop(0, n_kc, score_chunk,
                           (jnp.full((1, tq), jnp.inf, F32), jnp.full((1, tq), NEG_INF, F32)))

    cc = 2 * kc if (s_len // kc) % 2 == 0 else kc
    n_cc = (n_kc * kc + cc - 1) // cc

    @pl.when(n_cc * cc > n_kc * kc)
    def _():
        sc_s[pl.ds(pl.multiple_of(n_kc * kc, kc), kc), :] = jnp.full((kc, tq), NEG_INF, F32)

    def count_ge(thr):
        def body(j, c):
            rows = pl.ds(pl.multiple_of(j * cc, cc), cc)
            hit = jnp.where(sc_s[rows, :] >= thr, 1.0, 0.0)
            return c + jnp.sum(hit.reshape(cc // 8, 8, tq), axis=0)
        return jnp.sum(lax.fori_loop(0, n_cc, body, jnp.zeros((8, tq), F32)), axis=0, keepdims=True)

    def halve(state):
        lo_, hi_, c_lo = state
        mid = 0.5 * (lo_ + hi_)
        c_mid = count_ge(mid)
        ok = c_mid >= n_sel
        return jnp.where(ok, mid, lo_), jnp.where(ok, hi_, mid), jnp.where(ok, c_mid, c_lo)

    def search_more(carry):
        it, (_, _, c_lo) = carry
        return jnp.logical_and(it < n_bisect, jnp.max(c_lo) > n_sel)

    def search(carry):
        it, state = carry
        return it + 2, halve(halve(state))

    n_adm = (q_pos + 1).astype(F32)
    _, (lo, hi, _) = lax.while_loop(search_more, search, (jnp.int32(0), (lo, hi, n_adm)))

    qscale = (HEAD_DIM ** -0.5) * float(np.log2(np.e))
    q_all = jnp.concatenate(
        [(rope128(q_ref[:, h * HEAD_DIM:(h + 1) * HEAD_DIM], cq_ref[...], sq_ref[...]) * qscale).astype(BF16)
         for h in range(SA_HEADS)], axis=0)
    acc_s[...] = jnp.zeros_like(acc_s)

    def att_chunk(j, carry):
        m_, l_ = carry
        rows = pl.ds(pl.multiple_of(j * kc, kc), kc)
        bias = jnp.where(sc_s[rows, :] >= lo, 0.0, NEG_INF)
        logit = _dot_nt(kr_s[rows, :], q_all) + jnp.concatenate([bias] * SA_HEADS, axis=1)
        m_new = jnp.maximum(m_, jnp.max(logit, axis=0, keepdims=True))
        m_use = jnp.where(m_new == NEG_INF, 0.0, m_new)
        alpha = jnp.exp2(m_ - m_use)
        p = jnp.exp2(logit - m_use)
        acc_s[...] = alpha * acc_s[...] + _dot(vt_s[j], p.astype(BF16))
        return m_new, alpha * l_ + jnp.sum(p, axis=0, keepdims=True)

    _, l = lax.fori_loop(0, n_kc, att_chunk,
                         (jnp.full((1, SA_HEADS * tq), NEG_INF, F32), jnp.zeros((1, SA_HEADS * tq), F32)))
    out_t = acc_s[...] / l
    for h in range(SA_HEADS):
        o_ref[:, h * HEAD_DIM:(h + 1) * HEAD_DIM] = out_t[:, h * tq:(h + 1) * tq].T.astype(o_ref.dtype)


def _rope_tables(s_, dim, reps):
    inv_freq = ROPE_THETA ** (-jnp.arange(0, dim, 2, dtype=F32) / dim)
    ang = jnp.arange(s_, dtype=F32)[:, None] * inv_freq[None, :]
    cos, sin = jnp.cos(ang), jnp.sin(ang)
    return (jnp.tile(jnp.concatenate([cos, cos], axis=-1), (1, reps)),
            jnp.tile(jnp.concatenate([-sin, sin], axis=-1), (1, reps)))


def _dsa(hp, b_, s_, tq, kc):
    n_sel = min(INDEX_TOPK, s_ // 4)
    cos_h, sin_h = _rope_tables(s_, HEAD_DIM, 1)
    cos_i, sin_i = _rope_tables(s_, IDX_DIM, 2)
    nq = s_ // tq
    qrow = lambda b, i: b * nq + i
    tab_q = pl.BlockSpec((tq, HEAD_DIM), lambda b, i: (i, 0))
    tab_k = pl.BlockSpec((s_, HEAD_DIM), lambda b, i: (0, 0))
    return pl.pallas_call(
        functools.partial(_dsa_kernel, kc=kc, n_sel=n_sel, n_bisect=40),
        grid=(b_, nq),
        in_specs=[pl.BlockSpec((tq, IDX_HEADS * IDX_DIM), lambda b, i: (qrow(b, i), OFF_IQ // (IDX_HEADS * IDX_DIM))),
                  pl.BlockSpec((tq, D_C), lambda b, i: (qrow(b, i), OFF_CQ // D_C)),
                  pl.BlockSpec((tq, HEAD_DIM), lambda b, i: (qrow(b, i), OFF_GATE // HEAD_DIM)),
                  pl.BlockSpec((s_, HEAD_DIM), lambda b, i: (b, OFF_CK // HEAD_DIM)),
                  pl.BlockSpec((s_, HEAD_DIM), lambda b, i: (b, OFF_CV // HEAD_DIM)),
                  pl.BlockSpec((s_, HEAD_DIM), lambda b, i: (b, OFF_GATE // HEAD_DIM)),
                  tab_q, tab_q, tab_q, tab_q, tab_k, tab_k, tab_k, tab_k],
        out_specs=pl.BlockSpec((tq, D_C), lambda b, i: (qrow(b, i), 0)),
        out_shape=jax.ShapeDtypeStruct((b_ * s_, D_C), BF16),
        scratch_shapes=[pltpu.VMEM((s_, HEAD_DIM), BF16),
                        pltpu.VMEM((s_ // kc, HEAD_DIM, kc), BF16),
                        pltpu.VMEM((s_, HEAD_DIM), BF16),
                        pltpu.VMEM((s_, HEAD_DIM), BF16),
                        pltpu.VMEM((s_, tq), F32),
                        pltpu.VMEM((HEAD_DIM, SA_HEADS * tq), F32)],
        compiler_params=_cparams(("parallel", "arbitrary")),
        name="dsa",
    )(hp, hp, hp, hp, hp, hp, cos_h, sin_h, cos_i, sin_i, cos_h, sin_h, cos_i, sin_i)


def _cross_kernel(x_ref, a_ref, b_ref, c_ref, wmix_ref, g_ref, kv_ref, wq_ref, wo_ref, o_ref):
    x = (x_ref[...] + _dot(a_ref[...], wmix_ref[:D_A, :]) + _dot(b_ref[...], wmix_ref[D_A:D_A + D_B, :])
         + _dot(c_ref[...], wmix_ref[D_A + D_B:, :]))
    xn = _rms_rows(x, g_ref[...]).astype(BF16)
    q = _dot(xn, wq_ref[...])
    outs = []
    for h in range(MEM_HEADS):
        ls = slice(h * HEAD_DIM, (h + 1) * HEAD_DIM)
        kh = kv_ref[:, ls]
        vh = kv_ref[:, D_MEM + h * HEAD_DIM:D_MEM + (h + 1) * HEAD_DIM]
        logit = _dot_nt(q[:, ls].astype(BF16), kh) * (HEAD_DIM ** -0.5)
        p = jnp.exp(logit - jnp.max(logit, axis=-1, keepdims=True))
        l = jnp.sum(p, axis=-1, keepdims=True)
        outs.append(_dot(p.astype(BF16), vh) / l)
    o = jnp.concatenate(outs, axis=-1).astype(BF16)
    o_ref[...] = x + _dot(o, wo_ref[...])


def _mix_cross(x, oa, ob, oc, w_out, g, kv, w_q, w_o, l, b_, s_, tm):
    d = x.shape[1]
    m_len = kv.shape[0] // b_
    nt = s_ // tm
    row = lambda w: pl.BlockSpec((tm, w), lambda b, i: (b * nt + i, 0))
    return pl.pallas_call(
        _cross_kernel,
        grid=(b_, nt),
        in_specs=[row(d), row(D_A), row(D_B), row(D_C),
                  pl.BlockSpec((None, D_A + D_B + D_C, d), lambda b, i: (l, 0, 0)),
                  pl.BlockSpec((1, d), lambda b, i: (0, 0)),
                  pl.BlockSpec((m_len, 2 * D_MEM), lambda b, i: (b, 0)),
                  pl.BlockSpec((None, d, D_MEM), lambda b, i: (l, 0, 0)),
                  pl.BlockSpec((None, D_MEM, d), lambda b, i: (l, 0, 0))],
        out_specs=row(d),
        out_shape=jax.ShapeDtypeStruct(x.shape, F32),
        compiler_params=_cparams(("parallel", "parallel")),
        name="mix_cross",
    )(x, oa, ob, oc, w_out, g.reshape(1, d), kv, w_q, w_o)


def _mlp_kernel(x_ref, g_ref, wu_ref, wd_ref, gf_ref, o_ref, xn_ref, acc_ref, *, final_norm):
    j = pl.program_id(1)

    @pl.when(j == 0)
    def _():
        xn_ref[...] = _rms_rows(x_ref[...], g_ref[...]).astype(BF16)
        acc_ref[...] = jnp.zeros_like(acc_ref)

    hid = jnp.maximum(_dot(xn_ref[...], wu_ref[...]), 0.0)
    acc_ref[...] += _dot((hid * hid).astype(BF16), wd_ref[...])

    @pl.when(j == pl.num_programs(1) - 1)
    def _():
        y = x_ref[...] + acc_ref[...]
        o_ref[...] = _rms_rows(y, gf_ref[...]) if final_norm else y


def _mlp(x, g, w_up, w_down, g_final, l, tm, tf, final_norm):
    t, d = x.shape
    f = w_up.shape[2]
    return pl.pallas_call(
        functools.partial(_mlp_kernel, final_norm=final_norm),
        grid=(t // tm, f // tf),
        in_specs=[pl.BlockSpec((tm, d), lambda i, j: (i, 0)),
                  pl.BlockSpec((1, d), lambda i, j: (0, 0)),
                  pl.BlockSpec((None, d, tf), lambda i, j: (l, 0, j)),
                  pl.BlockSpec((None, tf, d), lambda i, j: (l, j, 0)),
                  pl.BlockSpec((1, d), lambda i, j: (0, 0))],
        out_specs=pl.BlockSpec((tm, d), lambda i, j: (i, 0)),
        out_shape=jax.ShapeDtypeStruct((t, d), F32),
        scratch_shapes=[pltpu.VMEM((tm, d), BF16), pltpu.VMEM((tm, d), F32)],
        compiler_params=_cparams(("parallel", "arbitrary")),
        name="mlp",
    )(x, g.reshape(1, d), w_up, w_down, g_final.reshape(1, d))


def _w_in_moves():
    widths = (3 * D_A, D_A, GDN_HEADS, GDN_HEADS, D_B, D_B, D_C, HEAD_DIM, HEAD_DIM,
              IDX_HEADS * IDX_DIM, IDX_DIM, IDX_HEADS)
    a_qkv, a_z, a_dec, _, b_u, b_v, c_q, c_k, c_v, i_q, i_k, _ = (int(v) for v in np.cumsum((0,) + widths)[:-1])
    return ((i_q, IDX_HEADS * IDX_DIM, OFF_IQ), (b_u, D_B, OFF_BU), (b_v, D_B, OFF_BV), (c_k, HEAD_DIM, OFF_CK),
            (c_v, HEAD_DIM, OFF_CV), (c_q, D_C, OFF_CQ), (a_qkv, 3 * D_A, OFF_AQKV), (a_z, D_A, OFF_AZ),
            (i_k, IDX_DIM + IDX_HEADS, OFF_GATE + GATE_IK), (a_dec, 2 * GDN_HEADS, OFF_GATE + GATE_DEC))


def _prep_w_in_kernel(w_ref, o_ref):
    o_ref[:, OFF_GATE:] = jnp.zeros((o_ref.shape[0], D_IN_PAD - OFF_GATE), o_ref.dtype)
    for src, width, dst in _w_in_moves():
        o_ref[:, dst:dst + width] = w_ref[:, src:src + width].astype(o_ref.dtype)


def _prep_w_in(w_in, dtype=BF16):
    depth, d, n = w_in.shape
    rt = _tile(d, 256)
    return pl.pallas_call(
        _prep_w_in_kernel,
        grid=(depth, d // rt),
        in_specs=[pl.BlockSpec((None, rt, n), lambda l, i: (l, i, 0))],
        out_specs=pl.BlockSpec((None, rt, D_IN_PAD), lambda l, i: (l, i, 0)),
        out_shape=jax.ShapeDtypeStruct((depth, d, D_IN_PAD), dtype),
        compiler_params=_cparams(("parallel", "parallel")),
        name="prep_w_in",
    )(w_in)


def _tile(n, pref):
    t = min(n, pref)
    while n % t:
        t //= 2
    return t


def kernel(x, mem, ln_mix, w_in, conv_w, a_log, dt_bias, gdn_norm, sg_ln_g, sg_ln_b, sg_w, sg_b, w_out, ln_cross,
           ln_mem, w_q_mem, w_kv_mem, w_o_mem, ln_mlp, w_up, w_down, ln_final):
    b_, s_, d = x.shape
    m_len = mem.shape[1]
    depth = w_in.shape[0]
    t = b_ * s_
    tm = _tile(t, 512)
    xf = x.reshape(t, d)
    memf = mem.reshape(b_ * m_len, d)
    w_in_p = _prep_w_in(w_in)
    w_out_b, w_kv_b, w_q_b, w_o_b = (w.astype(BF16) for w in (w_out, w_kv_mem, w_q_mem, w_o_mem))
    w_up_b, w_down_b = w_up.astype(BF16), w_down.astype(BF16)
    for l in range(depth):
        hp = _norm_proj(xf, ln_mix[l], w_in_p, l, _tile(t, 1024), D_IN_PAD // 5, F32)
        oa = _gdn(hp, conv_w[l], a_log[l], dt_bias[l], gdn_norm[l], b_, s_)
        ob = _sg(hp, sg_ln_g[l], sg_ln_b[l], sg_w[l], sg_b[l], _tile(s_, 512))
        oc = _dsa(hp, b_, s_, _tile(s_, 256), _tile(s_, 256))
        kv = _norm_proj(memf, ln_mem[l], w_kv_b, l, _tile(b_ * m_len, 512), 2 * D_MEM, BF16)
        xf = _mix_cross(xf, oa, ob, oc, w_out_b, ln_cross[l], kv, w_q_b, w_o_b, l, b_, s_, _tile(s_, 512))
        xf = _mlp(xf, ln_mlp[l], w_up_b, w_down_b, ln_final, l, tm, 1024, final_norm=(l == depth - 1))
    return xf.reshape(b_, s_, d)
```

```python
import functools

import jax
import jax.numpy as jnp
import numpy as np
from jax import lax
from jax.experimental import pallas as pl
from jax.experimental.pallas import tpu as pltpu

F32 = jnp.float32
BF16 = jnp.bfloat16

D_MODEL = 2048
HEAD_DIM = 128
GDN_HEADS = 6
GDN_CHUNK = 128
INV_BASE = 8
GDN_GROUP_ROWS = 512
CONV_K = 4
CONV_PAD = 8
SG_GROUPS = 4
SG_CHUNK = 128
SA_HEADS = 6
IDX_HEADS = 16
IDX_DIM = 64
INDEX_TOPK = 256
MEM_HEADS = 4
D_FF = 4 * D_MODEL
ROPE_THETA = 10000.0
NORM_EPS = 1e-6

D_A = GDN_HEADS * HEAD_DIM
D_B = SG_GROUPS * HEAD_DIM
D_C = SA_HEADS * HEAD_DIM
D_MEM = MEM_HEADS * HEAD_DIM

OFF_IQ = 0
OFF_BU = 1024
OFF_BV = 1536
OFF_CK = 2048
OFF_CV = 2176
OFF_CQ = 2304
OFF_AQKV = 3072
OFF_AZ = 5376
OFF_GATE = 6144
D_IN_PAD = 6400
GATE_IK = 0
GATE_IW = IDX_DIM
GATE_DEC = IDX_DIM + IDX_HEADS
GATE_BETA = GATE_DEC + GDN_HEADS

VMEM_LIMIT = 56 * 1024 * 1024
NEG_INF = float("-inf")


def _cparams(sem):
    return pltpu.CompilerParams(dimension_semantics=sem, vmem_limit_bytes=VMEM_LIMIT)


def _rms_rows(x, g):
    ms = jnp.mean(x * x, axis=-1, keepdims=True)
    return x * lax.rsqrt(ms + NORM_EPS) * g


def _dot(a, b):
    return jnp.dot(a, b, preferred_element_type=F32)


def _dot_nt(a, b, precision=None):
    return lax.dot_general(a, b, (((1,), (1,)), ((), ())), preferred_element_type=F32, precision=precision)


def _dot_tn(a, b):
    return lax.dot_general(a, b, (((0,), (0,)), ((), ())), preferred_element_type=F32)


def _bmm(a, b, precision=None):
    return lax.dot_general(a, b, (((2,), (1,)), ((0,), (0,))), preferred_element_type=F32, precision=precision)


def _bmm_nt(a, b, precision=None):
    return lax.dot_general(a, b, (((2,), (2,)), ((0,), (0,))), preferred_element_type=F32, precision=precision)


def _sigmoid(x):
    return 1.0 / (1.0 + jnp.exp(-x))


def _norm_proj_kernel(x_ref, g_ref, w_ref, o_ref, xn_ref):
    @pl.when(pl.program_id(1) == 0)
    def _():
        xn_ref[...] = _rms_rows(x_ref[...], g_ref[...]).astype(BF16)

    o_ref[...] = _dot(xn_ref[...], w_ref[...]).astype(o_ref.dtype)


def _norm_proj(x, g, w, l, tm, tn, out_dtype):
    t, d = x.shape
    n = w.shape[2]
    return pl.pallas_call(
        _norm_proj_kernel,
        grid=(t // tm, n // tn),
        in_specs=[pl.BlockSpec((tm, d), lambda i, j: (i, 0)),
                  pl.BlockSpec((1, d), lambda i, j: (0, 0)),
                  pl.BlockSpec((None, d, tn), lambda i, j: (l, 0, j))],
        out_specs=pl.BlockSpec((tm, tn), lambda i, j: (i, j)),
        out_shape=jax.ShapeDtypeStruct((t, n), out_dtype),
        scratch_shapes=[pltpu.VMEM((tm, d), BF16)],
        compiler_params=_cparams(("parallel", "arbitrary")),
        name="norm_proj",
    )(x, g.reshape(1, d), w)


def _gdn_kernel(alog_ref, dtb_ref, q_ref, k_ref, v_ref, z_ref, gate_ref, cwq_ref, cwk_ref, cwv_ref, gn_ref,
                o_ref, u_s, w_s, qd_s, kd_s, at_s, gl_s, o_s, pad_s, *, group, heads):
    s_len = q_ref.shape[0]
    c_len = GDN_CHUNK
    n_chunks = s_len // c_len
    row = lax.broadcasted_iota(jnp.int32, (s_len, 1), 0)
    pos = row % c_len
    lane = lax.broadcasted_iota(jnp.int32, (1, HEAD_DIM), 1)
    lane3 = lax.broadcasted_iota(jnp.int32, (1, 1, HEAD_DIM), 2)
    ci = lax.broadcasted_iota(jnp.int32, (c_len, c_len), 0)
    cj = lax.broadcasted_iota(jnp.int32, (c_len, c_len), 1)
    tril = (cj <= ci)[None]
    strict = (cj < ci)[None]
    base_blk = (ci // INV_BASE == cj // INV_BASE)[None]
    merge_blks = []
    size = INV_BASE
    while size < c_len:
        merge_blks.append(((ci // (2 * size) == cj // (2 * size)) & (ci // size > cj // size))[None])
        size *= 2
    rows_g = group * c_len

    pad_s[0:CONV_PAD, :] = jnp.zeros((CONV_PAD, HEAD_DIM), F32)

    def conv_silu(x, cw):
        pad_s[CONV_PAD:, :] = x
        y = x * cw[CONV_K - 1:CONV_K, :]
        for j in range(1, CONV_K):
            y = y + pad_s[CONV_PAD - j:CONV_PAD - j + s_len, :] * cw[CONV_K - 1 - j:CONV_K - j, :]
        return y * _sigmoid(y)

    def l2n(x):
        return x * lax.rsqrt(jnp.sum(x * x, axis=-1, keepdims=True) + NORM_EPS)

    gate = gate_ref[...]
    for hl in range(heads):
        h = pl.program_id(1) * heads + hl
        cs = slice(hl * HEAD_DIM, (hl + 1) * HEAD_DIM)
        q = l2n(conv_silu(q_ref[:, cs], cwq_ref[:, cs])) * (HEAD_DIM ** -0.5)
        k = l2n(conv_silu(k_ref[:, cs], cwk_ref[:, cs]))
        v = conv_silu(v_ref[:, cs], cwv_ref[:, cs])

        dec = jnp.sum(jnp.where(lane == GATE_DEC + h, gate, 0.0), axis=-1, keepdims=True)
        bet = jnp.sum(jnp.where(lane == GATE_BETA + h, gate, 0.0), axis=-1, keepdims=True)
        beta = jnp.broadcast_to(_sigmoid(bet), (s_len, HEAD_DIM))
        a_neg = -jnp.exp(jnp.full((1, 1), alog_ref[h], F32))
        xg = dec + dtb_ref[h]
        softplus = jnp.maximum(xg, 0.0) + jnp.log(1.0 + jnp.exp(-jnp.abs(xg)))
        g = jnp.broadcast_to(a_neg * softplus, (s_len, HEAD_DIM))

        gc = g
        shift = 1
        while shift < c_len:
            gc = gc + jnp.where(pos >= shift, pltpu.roll(gc, shift, axis=0), 0.0)
            shift *= 2

        kb = k * beta
        eg = jnp.exp(gc)
        vb = v * beta
        kbe = kb * eg
        qd_s[hl] = (q * eg).astype(BF16)

        for gi in range(n_chunks // group):
            sl = slice(gi * rows_g, (gi + 1) * rows_g)

            def r3(t):
                return t[sl].reshape(group, c_len, t.shape[-1])

            gc3 = r3(gc)
            k3 = r3(k)
            pmat = jnp.where(lane3 == 0, gc3, jnp.where(lane3 == 1, 1.0, 0.0))
            qmat = jnp.where(lane3 == 0, 1.0, jnp.where(lane3 == 1, -gc3, 0.0))
            dlog = _bmm_nt(pmat, qmat, precision=lax.Precision.HIGHEST)
            decay = jnp.where(tril, jnp.exp(jnp.where(tril, dlog, 0.0)), 0.0)
            k3b = k3.astype(BF16)
            amat = jnp.where(strict, _bmm_nt(r3(kb).astype(BF16), k3b) * decay, 0.0)
            attn = _bmm_nt(r3(q).astype(BF16), k3b) * decay
            bpow = jnp.where(base_blk, -amat, 0.0)
            tlow = bpow
            span = 2
            while span < INV_BASE:
                bb = bpow.astype(BF16)
                bpow = _bmm(bb, bb)
                tlow = tlow + bpow + _bmm(bpow.astype(BF16), tlow.astype(BF16))
                span *= 2
            for off_blk in merge_blks:
                a_off = jnp.where(off_blk, amat, 0.0).astype(BF16)
                tb = tlow.astype(BF16)
                xmat = a_off + _bmm(tb, a_off)
                tlow = tlow - xmat - _bmm(xmat.astype(BF16), tb)
            rhs = jnp.concatenate([r3(vb), r3(kbe)], axis=-1)
            sol = rhs + _bmm(tlow.astype(BF16), rhs.astype(BF16))
            glast = gc3[:, c_len - 1:c_len, :]
            u_s[hl, sl, :] = sol[:, :, :HEAD_DIM].reshape(rows_g, HEAD_DIM)
            w_s[hl, sl, :] = sol[:, :, HEAD_DIM:].reshape(rows_g, HEAD_DIM).astype(BF16)
            kd_s[hl, sl, :] = (k3 * jnp.exp(glast - gc3)).reshape(rows_g, HEAD_DIM).astype(BF16)
            at_s[hl, sl, :] = attn.reshape(rows_g, c_len).astype(BF16)
            gl_s[hl, gi * group * 8:(gi + 1) * group * 8, :] = jnp.broadcast_to(
                jnp.exp(glast), (group, 8, HEAD_DIM)).reshape(group * 8, HEAD_DIM)

    def step(n, states):
        rows = pl.ds(pl.multiple_of(n * c_len, c_len), c_len)
        new_states = []
        for hl in range(heads):
            sb = states[hl].astype(BF16)
            v_new = u_s[hl, rows, :] - _dot(w_s[hl, rows, :], sb)
            vnb = v_new.astype(BF16)
            o_s[hl, rows, :] = _dot(qd_s[hl, rows, :], sb) + _dot(at_s[hl, rows, :], vnb)
            gl = jnp.broadcast_to(gl_s[hl, pl.ds(pl.multiple_of(n * 8, 8), 1), :], (HEAD_DIM, HEAD_DIM))
            new_states.append(states[hl] * gl + _dot_tn(kd_s[hl, rows, :], vnb))
        return tuple(new_states)

    lax.fori_loop(0, n_chunks, step, tuple(jnp.zeros((HEAD_DIM, HEAD_DIM), F32) for _ in range(heads)))

    for hl in range(heads):
        cs = slice(hl * HEAD_DIM, (hl + 1) * HEAD_DIM)
        z = z_ref[:, cs]
        o_ref[:, cs] = (_rms_rows(o_s[hl], gn_ref[...]) * (z * _sigmoid(z))).astype(o_ref.dtype)


def _gdn(hp, conv_w, a_log, dt_bias, gdn_norm, b_, s_, heads=2):
    n_chunks = s_ // GDN_CHUNK
    group = min(GDN_GROUP_ROWS // GDN_CHUNK, n_chunks)
    hw = heads * HEAD_DIM
    hb = lambda off: (lambda b, h: (b, off // hw + h))
    cb = lambda off: (lambda b, h: (0, off // hw + h))
    smem = pl.BlockSpec(memory_space=pltpu.SMEM)
    return pl.pallas_call(
        functools.partial(_gdn_kernel, group=group, heads=heads),
        grid=(b_, GDN_HEADS // heads),
        in_specs=[smem, smem,
                  pl.BlockSpec((s_, hw), hb(OFF_AQKV)),
                  pl.BlockSpec((s_, hw), hb(OFF_AQKV + D_A)),
                  pl.BlockSpec((s_, hw), hb(OFF_AQKV + 2 * D_A)),
                  pl.BlockSpec((s_, hw), hb(OFF_AZ)),
                  pl.BlockSpec((s_, HEAD_DIM), lambda b, h: (b, OFF_GATE // HEAD_DIM)),
                  pl.BlockSpec((CONV_K, hw), cb(0)),
                  pl.BlockSpec((CONV_K, hw), cb(D_A)),
                  pl.BlockSpec((CONV_K, hw), cb(2 * D_A)),
                  pl.BlockSpec((1, HEAD_DIM), lambda b, h: (0, 0))],
        out_specs=pl.BlockSpec((s_, hw), lambda b, h: (b, h)),
        out_shape=jax.ShapeDtypeStruct((b_ * s_, D_A), BF16),
        scratch_shapes=[pltpu.VMEM((heads, s_, HEAD_DIM), F32),
                        pltpu.VMEM((heads, s_, HEAD_DIM), BF16),
                        pltpu.VMEM((heads, s_, HEAD_DIM), BF16),
                        pltpu.VMEM((heads, s_, HEAD_DIM), BF16),
                        pltpu.VMEM((heads, s_, GDN_CHUNK), BF16),
                        pltpu.VMEM((heads, n_chunks * 8, HEAD_DIM), F32),
                        pltpu.VMEM((heads, s_, HEAD_DIM), F32),
                        pltpu.VMEM((CONV_PAD + s_, HEAD_DIM), F32)],
        compiler_params=_cparams(("parallel", "arbitrary")),
        name="gdn",
    )(a_log, dt_bias, hp, hp, hp, hp, hp, conv_w, conv_w, conv_w, gdn_norm.reshape(1, HEAD_DIM))


def _gelu_tanh(x):
    c = float(np.sqrt(2.0 / np.pi))
    return 0.5 * x * (1.0 + jnp.tanh(c * (x + 0.044715 * (x * x * x))))


def _sg_kernel(u_ref, v_ref, lg_ref, lb_ref, w_ref, bt_ref, o_ref):
    ts = u_ref.shape[0]
    v = _gelu_tanh(v_ref[...])
    mu = jnp.mean(v, axis=-1, keepdims=True)
    vc = v - mu
    var = jnp.mean(vc * vc, axis=-1, keepdims=True)
    vn = (vc * lax.rsqrt(var + NORM_EPS) * lg_ref[...] + lb_ref[...]).astype(BF16)
    ti = lax.broadcasted_iota(jnp.int32, (SG_CHUNK, SG_CHUNK), 0)
    si = lax.broadcasted_iota(jnp.int32, (SG_CHUNK, SG_CHUNK), 1)
    bt = bt_ref[...]
    for g in range(SG_GROUPS):
        wg = jnp.where(si <= ti, w_ref[g], 0.0).astype(BF16)
        bias = jnp.broadcast_to(bt[:, g:g + 1], (SG_CHUNK, HEAD_DIM))
        cs = slice(g * HEAD_DIM, (g + 1) * HEAD_DIM)
        for c in range(ts // SG_CHUNK):
            rs = slice(c * SG_CHUNK, (c + 1) * SG_CHUNK)
            s = _dot(wg, vn[rs, cs]) + bias
            o_ref[rs, cs] = (_gelu_tanh(u_ref[rs, cs]) * s).astype(o_ref.dtype)


def _sg(hp, ln_g, ln_b, w_s, b_s, ts):
    t = hp.shape[0]
    return pl.pallas_call(
        _sg_kernel,
        grid=(t // ts,),
        in_specs=[pl.BlockSpec((ts, D_B), lambda i: (i, OFF_BU // D_B)),
                  pl.BlockSpec((ts, D_B), lambda i: (i, OFF_BV // D_B)),
                  pl.BlockSpec((1, D_B), lambda i: (0, 0)),
                  pl.BlockSpec((1, D_B), lambda i: (0, 0)),
                  pl.BlockSpec((SG_GROUPS, SG_CHUNK, SG_CHUNK), lambda i: (0, 0, 0)),
                  pl.BlockSpec((SG_CHUNK, SG_GROUPS), lambda i: (0, 0))],
        out_specs=pl.BlockSpec((ts, D_B), lambda i: (i, 0)),
        out_shape=jax.ShapeDtypeStruct((t, D_B), BF16),
        compiler_params=_cparams(("parallel",)),
        name="spatial_gating",
    )(hp, hp, ln_g.reshape(1, D_B), ln_b.reshape(1, D_B), w_s, b_s.T)


def _dsa_kernel(iq_ref, q_ref, gq_ref, k_ref, v_ref, gk_ref, cq_ref, sq_ref, ciq_ref, siq_ref,
                ck_ref, sk_ref, cik_ref, sik_ref, o_ref,
                kr_s, vt_s, iklo_s, ikhi_s, sc_s, acc_s, *, kc, n_sel, n_bisect):
    qi = pl.program_id(1)
    tq = q_ref.shape[0]
    s_len = k_ref.shape[0]
    lane = lax.broadcasted_iota(jnp.int32, (1, HEAD_DIM), 1)

    def rope128(x, cos, sin):
        return x * cos + pltpu.roll(x, HEAD_DIM // 2, axis=1) * sin

    def rope64(x, cos, sin):
        half = IDX_DIM // 2
        partner = jnp.where(lane % IDX_DIM < half, pltpu.roll(x, HEAD_DIM - half, axis=1),
                            pltpu.roll(x, half, axis=1))
        return x * cos + partner * sin

    @pl.when(qi == 0)
    def _():
        kr_s[...] = rope128(k_ref[...], ck_ref[...], sk_ref[...]).astype(BF16)
        for c in range(s_len // kc):
            vt_s[c] = v_ref[c * kc:(c + 1) * kc, :].T.astype(BF16)
        ik = jnp.where(lane < IDX_DIM, rope64(gk_ref[...], cik_ref[...], sik_ref[...]), 0.0)
        iklo_s[...] = ik.astype(BF16)
        ikhi_s[...] = pltpu.roll(ik, IDX_DIM, axis=1).astype(BF16)

    hsel = lax.broadcasted_iota(jnp.int32, (IDX_HEADS, HEAD_DIM), 0)
    lsel = lax.broadcasted_iota(jnp.int32, (IDX_HEADS, HEAD_DIM), 1)
    pick = (lsel == hsel + GATE_IW).astype(F32)
    iw_t = _dot_nt(pick, gq_ref[...], precision=lax.Precision.HIGHEST) * ((IDX_HEADS ** -0.5) * (IDX_DIM ** -0.5))

    n_kc = ((qi + 1) * tq + kc - 1) // kc
    q_pos = qi * tq + lax.broadcasted_iota(jnp.int32, (1, tq), 1)

    iq_pairs = []
    for p in range(IDX_HEADS // 2):
        ls = slice(p * HEAD_DIM, (p + 1) * HEAD_DIM)
        iq_pairs.append(rope64(iq_ref[:, ls], ciq_ref[...], siq_ref[...]).astype(BF16))

    def score_chunk(j, carry):
        smin, smax = carry
        r0 = pl.multiple_of(j * kc, kc)
        rows = pl.ds(r0, kc)
        ik_lo = iklo_s[rows, :]
        ik_hi = ikhi_s[rows, :]
        acc = jnp.zeros((kc, tq), F32)
        for hh in range(IDX_HEADS):
            ikx = ik_lo if hh % 2 == 0 else ik_hi
            rel = jnp.maximum(_dot_nt(ikx, iq_pairs[hh // 2]), 0.0)
            acc = acc + rel * iw_t[hh:hh + 1, :]
        k_pos = r0 + lax.broadcasted_iota(jnp.int32, (kc, 1), 0)
        adm = k_pos <= q_pos
        sc_s[rows, :] = jnp.where(adm, acc, NEG_INF)
        smin = jnp.minimum(smin, jnp.min(jnp.where(adm, acc, jnp.inf), axis=0, keepdims=True))
        smax = jnp.maximum(smax, jnp.max(jnp.where(adm, acc, NEG_INF), axis=0, keepdims=True))
        return smin, smax

    lo, hi = lax.fori_loop(0, n_kc, score_chunk,
                           (jnp.full((1, tq), jnp.inf, F32), jnp.full((1, tq), NEG_INF, F32)))

    def count_ge(thr):
        def body(j, c):
            rows = pl.ds(pl.multiple_of(j * kc, kc), kc)
            hit = jnp.where(sc_s[rows, :] >= thr, 1.0, 0.0)
            return c + jnp.sum(hit.reshape(kc // 8, 8, tq), axis=0)
        return jnp.sum(lax.fori_loop(0, n_kc, body, jnp.zeros((8, tq), F32)), axis=0, keepdims=True)

    def halve(state):
        lo_, hi_, c_lo = state
        mid = 0.5 * (lo_ + hi_)
        c_mid = count_ge(mid)
        ok = c_mid >= n_sel
        return jnp.where(ok, mid, lo_), jnp.where(ok, hi_, mid), jnp.where(ok, c_mid, c_lo)

    def search_more(carry):
        it, (_, _, c_lo) = carry
        return jnp.logical_and(it < n_bisect, jnp.max(c_lo) > n_sel)

    def search(carry):
        it, state = carry
        return it + 2, halve(halve(state))

    n_adm = (q_pos + 1).astype(F32)
    _, (lo, hi, _) = lax.while_loop(search_more, search, (jnp.int32(0), (lo, hi, n_adm)))

    qscale = (HEAD_DIM ** -0.5) * float(np.log2(np.e))
    q_all = jnp.concatenate(
        [(rope128(q_ref[:, h * HEAD_DIM:(h + 1) * HEAD_DIM], cq_ref[...], sq_ref[...]) * qscale).astype(BF16)
         for h in range(SA_HEADS)], axis=0)
    acc_s[...] = jnp.zeros_like(acc_s)

    def att_chunk(j, carry):
        m_, l_ = carry
        rows = pl.ds(pl.multiple_of(j * kc, kc), kc)
        bias = jnp.where(sc_s[rows, :] >= lo, 0.0, NEG_INF)
        logit = _dot_nt(kr_s[rows, :], q_all) + jnp.concatenate([bias] * SA_HEADS, axis=1)
        m_new = jnp.maximum(m_, jnp.max(logit, axis=0, keepdims=True))
        m_use = jnp.where(m_new == NEG_INF, 0.0, m_new)
        alpha = jnp.exp2(m_ - m_use)
        p = jnp.exp2(logit - m_use)
        acc_s[...] = alpha * acc_s[...] + _dot(vt_s[j], p.astype(BF16))
        return m_new, alpha * l_ + jnp.sum(p, axis=0, keepdims=True)

    _, l = lax.fori_loop(0, n_kc, att_chunk,
                         (jnp.full((1, SA_HEADS * tq), NEG_INF, F32), jnp.zeros((1, SA_HEADS * tq), F32)))
    out_t = acc_s[...] / l
    for h in range(SA_HEADS):
        o_ref[:, h * HEAD_DIM:(h + 1) * HEAD_DIM] = out_t[:, h * tq:(h + 1) * tq].T.astype(o_ref.dtype)


def _rope_tables(s_, dim, reps):
    inv_freq = ROPE_THETA ** (-jnp.arange(0, dim, 2, dtype=F32) / dim)
    ang = jnp.arange(s_, dtype=F32)[:, None] * inv_freq[None, :]
    cos, sin = jnp.cos(ang), jnp.sin(ang)
    return (jnp.tile(jnp.concatenate([cos, cos], axis=-1), (1, reps)),
            jnp.tile(jnp.concatenate([-sin, sin], axis=-1), (1, reps)))


def _dsa(hp, b_, s_, tq, kc):
    n_sel = min(INDEX_TOPK, s_ // 4)
    cos_h, sin_h = _rope_tables(s_, HEAD_DIM, 1)
    cos_i, sin_i = _rope_tables(s_, IDX_DIM, 2)
    nq = s_ // tq
    qrow = lambda b, i: b * nq + i
    tab_q = pl.BlockSpec((tq, HEAD_DIM), lambda b, i: (i, 0))
    tab_k = pl.BlockSpec((s_, HEAD_DIM), lambda b, i: (0, 0))
    return pl.pallas_call(
        functools.partial(_dsa_kernel, kc=kc, n_sel=n_sel, n_bisect=40),
        grid=(b_, nq),
        in_specs=[pl.BlockSpec((tq, IDX_HEADS * IDX_DIM), lambda b, i: (qrow(b, i), OFF_IQ // (IDX_HEADS * IDX_DIM))),
                  pl.BlockSpec((tq, D_C), lambda b, i: (qrow(b, i), OFF_CQ // D_C)),
                  pl.BlockSpec((tq, HEAD_DIM), lambda b, i: (qrow(b, i), OFF_GATE // HEAD_DIM)),
                  pl.BlockSpec((s_, HEAD_DIM), lambda b, i: (b, OFF_CK // HEAD_DIM)),
                  pl.BlockSpec((s_, HEAD_DIM), lambda b, i: (b, OFF_CV // HEAD_DIM)),
                  pl.BlockSpec((s_, HEAD_DIM), lambda b, i: (b, OFF_GATE // HEAD_DIM)),
                  tab_q, tab_q, tab_q, tab_q, tab_k, tab_k, tab_k, tab_k],
        out_specs=pl.BlockSpec((tq, D_C), lambda b, i: (qrow(b, i), 0)),
        out_shape=jax.ShapeDtypeStruct((b_ * s_, D_C), BF16),
        scratch_shapes=[pltpu.VMEM((s_, HEAD_DIM), BF16),
                        pltpu.VMEM((s_ // kc, HEAD_DIM, kc), BF16),
                        pltpu.VMEM((s_, HEAD_DIM), BF16),
                        pltpu.VMEM((s_, HEAD_DIM), BF16),
                        pltpu.VMEM((s_, tq), F32),
                        pltpu.VMEM((HEAD_DIM, SA_HEADS * tq), F32)],
        compiler_params=_cparams(("parallel", "arbitrary")),
        name="dsa",
    )(hp, hp, hp, hp, hp, hp, cos_h, sin_h, cos_i, sin_i, cos_h, sin_h, cos_i, sin_i)


def _cross_kernel(x_ref, a_ref, b_ref, c_ref, wmix_ref, g_ref, kv_ref, wq_ref, wo_ref, o_ref):
    x = (x_ref[...] + _dot(a_ref[...], wmix_ref[:D_A, :]) + _dot(b_ref[...], wmix_ref[D_A:D_A + D_B, :])
         + _dot(c_ref[...], wmix_ref[D_A + D_B:, :]))
    xn = _rms_rows(x, g_ref[...]).astype(BF16)
    q = _dot(xn, wq_ref[...])
    outs = []
    for h in range(MEM_HEADS):
        ls = slice(h * HEAD_DIM, (h + 1) * HEAD_DIM)
        kh = kv_ref[:, ls]
        vh = kv_ref[:, D_MEM + h * HEAD_DIM:D_MEM + (h + 1) * HEAD_DIM]
        logit = _dot_nt(q[:, ls].astype(BF16), kh) * (HEAD_DIM ** -0.5)
        p = jnp.exp(logit - jnp.max(logit, axis=-1, keepdims=True))
        l = jnp.sum(p, axis=-1, keepdims=True)
        outs.append(_dot(p.astype(BF16), vh) / l)
    o = jnp.concatenate(outs, axis=-1).astype(BF16)
    o_ref[...] = x + _dot(o, wo_ref[...])


def _mix_cross(x, oa, ob, oc, w_out, g, kv, w_q, w_o, l, b_, s_, tm):
    d = x.shape[1]
    m_len = kv.shape[0] // b_
    nt = s_ // tm
    row = lambda w: pl.BlockSpec((tm, w), lambda b, i: (b * nt + i, 0))
    return pl.pallas_call(
        _cross_kernel,
        grid=(b_, nt),
        in_specs=[row(d), row(D_A), row(D_B), row(D_C),
                  pl.BlockSpec((None, D_A + D_B + D_C, d), lambda b, i: (l, 0, 0)),
                  pl.BlockSpec((1, d), lambda b, i: (0, 0)),
                  pl.BlockSpec((m_len, 2 * D_MEM), lambda b, i: (b, 0)),
                  pl.BlockSpec((None, d, D_MEM), lambda b, i: (l, 0, 0)),
                  pl.BlockSpec((None, D_MEM, d), lambda b, i: (l, 0, 0))],
        out_specs=row(d),
        out_shape=jax.ShapeDtypeStruct(x.shape, F32),
        compiler_params=_cparams(("parallel", "parallel")),
        name="mix_cross",
    )(x, oa, ob, oc, w_out, g.reshape(1, d), kv, w_q, w_o)


def _mlp_kernel(x_ref, g_ref, wu_ref, wd_ref, gf_ref, o_ref, xn_ref, acc_ref, *, final_norm):
    j = pl.program_id(1)

    @pl.when(j == 0)
    def _():
        xn_ref[...] = _rms_rows(x_ref[...], g_ref[...]).astype(BF16)
        acc_ref[...] = jnp.zeros_like(acc_ref)

    hid = jnp.maximum(_dot(xn_ref[...], wu_ref[...]), 0.0)
    acc_ref[...] += _dot((hid * hid).astype(BF16), wd_ref[...])

    @pl.when(j == pl.num_programs(1) - 1)
    def _():
        y = x_ref[...] + acc_ref[...]
        o_ref[...] = _rms_rows(y, gf_ref[...]) if final_norm else y


def _mlp(x, g, w_up, w_down, g_final, l, tm, tf, final_norm):
    t, d = x.shape
    f = w_up.shape[2]
    return pl.pallas_call(
        functools.partial(_mlp_kernel, final_norm=final_norm),
        grid=(t // tm, f // tf),
        in_specs=[pl.BlockSpec((tm, d), lambda i, j: (i, 0)),
                  pl.BlockSpec((1, d), lambda i, j: (0, 0)),
                  pl.BlockSpec((None, d, tf), lambda i, j: (l, 0, j)),
                  pl.BlockSpec((None, tf, d), lambda i, j: (l, j, 0)),
                  pl.BlockSpec((1, d), lambda i, j: (0, 0))],
        out_specs=pl.BlockSpec((tm, d), lambda i, j: (i, 0)),
        out_shape=jax.ShapeDtypeStruct((t, d), F32),
        scratch_shapes=[pltpu.VMEM((tm, d), BF16), pltpu.VMEM((tm, d), F32)],
        compiler_params=_cparams(("parallel", "arbitrary")),
        name="mlp",
    )(x, g.reshape(1, d), w_up, w_down, g_final.reshape(1, d))


def _w_in_moves():
    widths = (3 * D_A, D_A, GDN_HEADS, GDN_HEADS, D_B, D_B, D_C, HEAD_DIM, HEAD_DIM,
              IDX_HEADS * IDX_DIM, IDX_DIM, IDX_HEADS)
    a_qkv, a_z, a_dec, _, b_u, b_v, c_q, c_k, c_v, i_q, i_k, _ = (int(v) for v in np.cumsum((0,) + widths)[:-1])
    return ((i_q, IDX_HEADS * IDX_DIM, OFF_IQ), (b_u, D_B, OFF_BU), (b_v, D_B, OFF_BV), (c_k, HEAD_DIM, OFF_CK),
            (c_v, HEAD_DIM, OFF_CV), (c_q, D_C, OFF_CQ), (a_qkv, 3 * D_A, OFF_AQKV), (a_z, D_A, OFF_AZ),
            (i_k, IDX_DIM + IDX_HEADS, OFF_GATE + GATE_IK), (a_dec, 2 * GDN_HEADS, OFF_GATE + GATE_DEC))


def _prep_w_in_kernel(w_ref, o_ref):
    o_ref[:, OFF_GATE:] = jnp.zeros((o_ref.shape[0], D_IN_PAD - OFF_GATE), o_ref.dtype)
    for src, width, dst in _w_in_moves():
        o_ref[:, dst:dst + width] = w_ref[:, src:src + width].astype(o_ref.dtype)


def _prep_w_in(w_in, dtype=BF16):
    depth, d, n = w_in.shape
    rt = _tile(d, 256)
    return pl.pallas_call(
        _prep_w_in_kernel,
        grid=(depth, d // rt),
        in_specs=[pl.BlockSpec((None, rt, n), lambda l, i: (l, i, 0))],
        out_specs=pl.BlockSpec((None, rt, D_IN_PAD), lambda l, i: (l, i, 0)),
        out_shape=jax.ShapeDtypeStruct((depth, d, D_IN_PAD), dtype),
        compiler_params=_cparams(("parallel", "parallel")),
        name="prep_w_in",
    )(w_in)


def _tile(n, pref):
    t = min(n, pref)
    while n % t:
        t //= 2
    return t


def kernel(x, mem, ln_mix, w_in, conv_w, a_log, dt_bias, gdn_norm, sg_ln_g, sg_ln_b, sg_w, sg_b, w_out, ln_cross,
           ln_mem, w_q_mem, w_kv_mem, w_o_mem, ln_mlp, w_up, w_down, ln_final):
    b_, s_, d = x.shape
    m_len = mem.shape[1]
    depth = w_in.shape[0]
    t = b_ * s_
    tm = _tile(t, 512)
    xf = x.reshape(t, d)
    memf = mem.reshape(b_ * m_len, d)
    w_in_p = _prep_w_in(w_in)
    w_out_b, w_kv_b, w_q_b, w_o_b = (w.astype(BF16) for w in (w_out, w_kv_mem, w_q_mem, w_o_mem))
    w_up_b, w_down_b = w_up.astype(BF16), w_down.astype(BF16)
    for l in range(depth):
        hp = _norm_proj(xf, ln_mix[l], w_in_p, l, _tile(t, 1024), D_IN_PAD // 5, F32)
        oa = _gdn(hp, conv_w[l], a_log[l], dt_bias[l], gdn_norm[l], b_, s_)
        ob = _sg(hp, sg_ln_g[l], sg_ln_b[l], sg_w[l], sg_b[l], _tile(s_, 512))
        oc = _dsa(hp, b_, s_, _tile(s_, 256), _tile(s_, 256))
        kv = _norm_proj(memf, ln_mem[l], w_kv_b, l, _tile(b_ * m_len, 512), 2 * D_MEM, BF16)
        xf = _mix_cross(xf, oa, ob, oc, w_out_b, ln_cross[l], kv, w_q_b, w_o_b, l, b_, s_, _tile(s_, 512))
        xf = _mlp(xf, ln_mlp[l], w_up_b, w_down_b, ln_final, l, tm, 1024, final_norm=(l == depth - 1))
    return xf.reshape(b_, s_, d)
```
